```python
import jax, jax.numpy as jnp
from jax import lax
import numpy as np

D_MODEL = 1024
BATCH = 16
SEQ = 2048
DEPTH = 2
DEC_BATCH = 32
DEC_SEQ = 32
PAST_LEN = 2048

CHUNK = 64
BLOCK = 16
GLA_HEADS = 4
GLA_DK = 48
GLA_DV = 96
GLA_GATE_RANK = 16
GLA_TAU = 16.0
HG_HEADS = 4
HG_DK = 64
HG_DV = 96
RG_WIDTH = 256
RG_BLOCKS = 4
RG_BLOCK_DIM = RG_WIDTH // RG_BLOCKS
RG_CONV = 4
RG_C = 8.0
MIX_WIDTH = GLA_HEADS * GLA_DV + HG_HEADS * HG_DV + RG_WIDTH
MEM_LEN = 256
XA_HEADS = 4
XA_DH = D_MODEL // XA_HEADS
D_FF = 2816
N_EXPERTS = 8
TOP_K = 2
N_DENSE = (DEPTH + 1) // 2
N_MOE = DEPTH // 2
EPS = 1e-6
IN_SIZES = (GLA_HEADS * GLA_DK, GLA_HEADS * GLA_DK, GLA_HEADS * GLA_DV, GLA_HEADS * GLA_DV, GLA_GATE_RANK,
            HG_HEADS * HG_DK, HG_HEADS * HG_DK, HG_HEADS * HG_DV, HG_HEADS * HG_DV,
            RG_WIDTH, RG_WIDTH)
IN_WIDTH = sum(IN_SIZES)

kernel_name = 'hymba_gla_hgrn2_rglru_stream_step'


def rms_norm(x, g):
    xf = x.astype(jnp.float32)
    y = xf * lax.rsqrt(jnp.mean(xf * xf, axis=-1, keepdims=True) + EPS)
    return (y * g.astype(jnp.float32)).astype(x.dtype)


def head_rms(o, g):
    B, T, H, V = o.shape
    of = o.astype(jnp.float32)
    of = of * lax.rsqrt(jnp.mean(of * of, axis=-1, keepdims=True) + EPS)
    return (of.reshape(B, T, H * V) * g.astype(jnp.float32)).astype(o.dtype)


def gated_recurrence(q, k, v, log_g, s0):
    B, T, H, K = q.shape
    V = v.shape[-1]
    f32 = jnp.float32
    pad = (-T) % BLOCK
    padt = lambda a: jnp.pad(a.astype(f32), ((0, 0), (0, pad), (0, 0), (0, 0)))
    n = (T + pad) // BLOCK
    blk = lambda a: padt(a).reshape(B, n, BLOCK, H, a.shape[-1])
    qb, kb, vb, lg = blk(q), blk(k), blk(v), blk(log_g)
    b = jnp.cumsum(lg, axis=2)
    causal = jnp.tril(jnp.ones((BLOCK, BLOCK), bool))[None, None, :, :, None, None]
    diff = b[:, :, :, None] - b[:, :, None]
    decay = jnp.where(causal, jnp.exp(jnp.where(causal, diff, 0.0)), 0.0)
    scores = jnp.einsum('bnthk,bnshk,bntshk->bntsh', qb, kb, decay)
    o_intra = jnp.einsum('bntsh,bnshv->bnthv', scores, vb)
    b_last = b[:, :, -1]
    k_dec = kb * jnp.exp(b_last[:, :, None] - b)
    q_dec = qb * jnp.exp(b)

    def step(S, inp):
        bl, kd, vv = inp
        S_new = jnp.exp(bl)[..., None] * S + jnp.einsum('blhk,blhv->bhkv', kd, vv)
        return S_new, S

    xs = (jnp.moveaxis(b_last, 1, 0), jnp.moveaxis(k_dec, 1, 0), jnp.moveaxis(vb, 1, 0))
    S_T, S_prev = lax.scan(step, s0.astype(f32), xs)
    o_inter = jnp.einsum('bnlhk,nbhkv->bnlhv', q_dec, S_prev)
    o = (o_intra + o_inter).reshape(B, n * BLOCK, H, V)[:, :T]
    return o.astype(v.dtype), S_T.astype(s0.dtype)


def gla_mixer(q, k, v, r, a_low, wa2, ba, onorm, s0):
    B, T, _ = q.shape
    q = q.reshape(B, T, GLA_HEADS, GLA_DK) * (GLA_DK ** -0.5)
    k = k.reshape(B, T, GLA_HEADS, GLA_DK)
    v = v.reshape(B, T, GLA_HEADS, GLA_DV)
    log_a = jax.nn.log_sigmoid((a_low @ wa2 + ba).astype(jnp.float32)) / GLA_TAU
    o, s = gated_recurrence(q, k, v, log_a.reshape(B, T, GLA_HEADS, GLA_DK), s0)
    return head_rms(o, onorm) * jax.nn.silu(r), s


def hgrn2_mixer(q, f_pre, i, g, lb, onorm, s0):
    B, T, _ = q.shape
    z = f_pre.astype(jnp.float32)
    sig = jax.nn.sigmoid(z)
    f = lb + (1.0 - lb) * sig
    log_f = jnp.log(f)
    k = (1.0 - lb) * jax.nn.sigmoid(-z)
    q = jax.nn.silu(q).reshape(B, T, HG_HEADS, HG_DK)
    o, s = gated_recurrence(q, k.reshape(B, T, HG_HEADS, HG_DK), i.reshape(B, T, HG_HEADS, HG_DV),
                            log_f.reshape(B, T, HG_HEADS, HG_DK), s0)
    return head_rms(o, onorm) * jax.nn.silu(g), s


def rglru_mixer(xb, gb, conv_w, conv_b, wr, br, wi, bi, lam, h0, conv0):
    B, T, W = xb.shape
    xc = jnp.concatenate([conv0.astype(xb.dtype), xb], axis=1)
    new_conv = xc[:, -(RG_CONV - 1):]
    u = conv_b
    for j in range(RG_CONV):
        u = u + xc[:, j:j + T] * conv_w[j]
    ub = u.reshape(B, T, RG_BLOCKS, RG_BLOCK_DIM)
    r = jax.nn.sigmoid(jnp.einsum('btnc,ncd->btnd', ub, wr).reshape(B, T, W) + br).astype(jnp.float32)
    ig = jax.nn.sigmoid(jnp.einsum('btnc,ncd->btnd', ub, wi).reshape(B, T, W) + bi).astype(jnp.float32)
    log_a = -RG_C * r * jax.nn.softplus(-lam.astype(jnp.float32))
    a = jnp.exp(log_a)
    beta = jnp.sqrt(jnp.maximum(-jnp.expm1(2.0 * log_a), 1e-12))
    bterm = beta * (ig * u.astype(jnp.float32))
    bterm = bterm.at[:, 0].add(a[:, 0] * h0.astype(jnp.float32))

    def comb(e1, e2):
        a1, b1 = e1
        a2, b2 = e2
        return a1 * a2, a2 * b1 + b2

    _, h = lax.associative_scan(comb, (a, bterm), axis=1)
    y = h.astype(xb.dtype) * jax.nn.gelu(gb)
    return y, h[:, -1].astype(h0.dtype), new_conv.astype(conv0.dtype)


def mem_kv(mem, g, wk, wv):
    B, M, _ = mem.shape
    m = rms_norm(mem, g)
    return (m @ wk).reshape(B, M, XA_HEADS, XA_DH), (m @ wv).reshape(B, M, XA_HEADS, XA_DH)


def mem_attend(h, k, v, wq, wo):
    B, T, _ = h.shape
    q = (h @ wq).reshape(B, T, XA_HEADS, XA_DH)
    s = jnp.einsum('bthd,bmhd->bhtm', q, k).astype(jnp.float32) * (XA_DH ** -0.5)
    p = jax.nn.softmax(s, axis=-1).astype(v.dtype)
    o = jnp.einsum('bhtm,bmhd->bthd', p, v).reshape(B, T, D_MODEL)
    return o @ wo


def swiglu(h, w1, w3, w2):
    return (jax.nn.silu(h @ w1) * (h @ w3)) @ w2


def moe(h, router, w1, w3, w2):
    logits = (h @ router).astype(jnp.float32)
    top_v, top_i = lax.top_k(logits, TOP_K)
    gates = jax.nn.softmax(top_v, axis=-1)
    dense_gate = jnp.sum(jax.nn.one_hot(top_i, N_EXPERTS, dtype=jnp.float32) * gates[..., None], axis=-2)
    out = jnp.zeros_like(h)
    for e in range(N_EXPERTS):
        out = out + dense_gate[..., e:e + 1].astype(h.dtype) * swiglu(h, w1[e], w3[e], w2[e])
    return out


def trunk(x, mem_k, mem_v, s_gla, s_hg, s_rg, s_conv, W):
    p = jax.nn.softmax(W['hg_lb'].astype(jnp.float32), axis=0)
    lb_all = jnp.cumsum(p, axis=0) - p[:1]
    points = [sum(IN_SIZES[:j + 1]) for j in range(len(IN_SIZES) - 1)]
    n_gla, n_hg, n_rg, n_cv = [], [], [], []
    for l in range(DEPTH):
        h = rms_norm(x, W['norm_mix'][l])
        gq, gk, gv, gr, ga, hq, hf, hi, hgate, rx, rgate = jnp.split(h @ W['w_in'][l], points, axis=-1)
        o_gla, sg = gla_mixer(gq, gk, gv, gr, ga, W['gla_wa2'][l], W['gla_ba'][l], W['gla_onorm'][l], s_gla[l])
        o_hg, sh = hgrn2_mixer(hq, hf, hi, hgate, lb_all[l], W['hg_onorm'][l], s_hg[l])
        o_rg, sr, sc = rglru_mixer(rx, rgate, W['rg_conv_w'][l], W['rg_conv_b'][l], W['rg_wr'][l], W['rg_br'][l],
                                   W['rg_wi'][l], W['rg_bi'][l], W['rg_lambda'][l], s_rg[l], s_conv[l])
        x = x + jnp.concatenate([o_gla, o_hg, o_rg], axis=-1) @ W['w_out'][l]
        h = rms_norm(x, W['norm_xattn'][l])
        x = x + mem_attend(h, mem_k[l], mem_v[l], W['xa_wq'][l], W['xa_wo'][l])
        h = rms_norm(x, W['norm_ffn'][l])
        if l % 2 == 0:
            j = l // 2
            x = x + swiglu(h, W['ffd_w1'][j], W['ffd_w3'][j], W['ffd_w2'][j])
        else:
            j = l // 2
            x = x + moe(h, W['moe_router'][j], W['moe_w1'][j], W['moe_w3'][j], W['moe_w2'][j])
        n_gla.append(sg)
        n_hg.append(sh)
        n_rg.append(sr)
        n_cv.append(sc)
    y = rms_norm(x, W['norm_final'])
    return y, jnp.stack(n_gla), jnp.stack(n_hg), jnp.stack(n_rg), jnp.stack(n_cv)


def setup_inputs(seed: int = 0) -> dict:
    key = jax.random.key(seed)
    keys = jax.random.split(key, 40)
    f32 = jnp.float32
    L = DEPTH

    def nrm(i, shape, scale):
        return jax.random.normal(keys[i], shape, f32) * scale

    def gain(i, shape):
        return 1.0 + nrm(i, shape, 0.02)

    u = jax.random.uniform(keys[26], (L, RG_WIDTH), f32, 0.9, 0.999)
    a8 = u ** (1.0 / RG_C)
    rg_lambda = jnp.log(a8) - jnp.log1p(-a8)
    return {
        'x_prompt': nrm(0, (BATCH, SEQ, D_MODEL), 1.0),
        'x_sample': nrm(1, (DEC_BATCH, DEC_SEQ, D_MODEL), 1.0),
        'state_gla': nrm(2, (L, DEC_BATCH, GLA_HEADS, GLA_DK, GLA_DV), 1.0),
        'state_hgrn': nrm(3, (L, DEC_BATCH, HG_HEADS, HG_DK, HG_DV), 0.5),
        'state_rglru': nrm(4, (L, DEC_BATCH, RG_WIDTH), 0.5),
        'state_conv': nrm(5, (L, DEC_BATCH, RG_CONV - 1, RG_WIDTH), 1.0),
        'cache_mem_k': nrm(6, (L, DEC_BATCH, MEM_LEN, XA_HEADS, XA_DH), 1.0),
        'cache_mem_v': nrm(7, (L, DEC_BATCH, MEM_LEN, XA_HEADS, XA_DH), 1.0),
        'mem_prompt': nrm(8, (BATCH, MEM_LEN, D_MODEL), 1.0),
        'norm_mix': gain(9, (L, D_MODEL)),
        'norm_xattn': gain(10, (L, D_MODEL)),
        'norm_ffn': gain(11, (L, D_MODEL)),
        'norm_mem': gain(12, (L, D_MODEL)),
        'norm_final': gain(13, (D_MODEL,)),
        'w_in': nrm(14, (L, D_MODEL, IN_WIDTH), D_MODEL ** -0.5),
        'gla_wa2': nrm(15, (L, GLA_GATE_RANK, GLA_HEADS * GLA_DK), GLA_GATE_RANK ** -0.5),
        'gla_ba': nrm(16, (L, GLA_HEADS * GLA_DK), 0.1),
        'gla_onorm': gain(17, (L, GLA_HEADS * GLA_DV)),
        'hg_lb': nrm(18, (L, HG_HEADS * HG_DK), 0.5),
        'hg_onorm': gain(19, (L, HG_HEADS * HG_DV)),
        'rg_conv_w': nrm(20, (L, RG_CONV, RG_WIDTH), RG_CONV ** -0.5),
        'rg_conv_b': nrm(21, (L, RG_WIDTH), 0.02),
        'rg_wr': nrm(22, (L, RG_BLOCKS, RG_BLOCK_DIM, RG_BLOCK_DIM), RG_BLOCK_DIM ** -0.5),
        'rg_br': nrm(23, (L, RG_WIDTH), 0.1),
        'rg_wi': nrm(24, (L, RG_BLOCKS, RG_BLOCK_DIM, RG_BLOCK_DIM), RG_BLOCK_DIM ** -0.5),
        'rg_bi': nrm(25, (L, RG_WIDTH), 0.1),
        'rg_lambda': rg_lambda,
        'w_out': nrm(27, (L, MIX_WIDTH, D_MODEL), MIX_WIDTH ** -0.5),
        'xa_wq': nrm(28, (L, D_MODEL, D_MODEL), D_MODEL ** -0.5),
        'xa_wk': nrm(29, (L, D_MODEL, D_MODEL), D_MODEL ** -0.5),
        'xa_wv': nrm(30, (L, D_MODEL, D_MODEL), D_MODEL ** -0.5),
        'xa_wo': nrm(31, (L, D_MODEL, D_MODEL), D_MODEL ** -0.5),
        'ffd_w1': nrm(32, (N_DENSE, D_MODEL, D_FF), D_MODEL ** -0.5),
        'ffd_w3': nrm(33, (N_DENSE, D_MODEL, D_FF), D_MODEL ** -0.5),
        'ffd_w2': nrm(34, (N_DENSE, D_FF, D_MODEL), D_FF ** -0.5),
        'moe_router': nrm(35, (N_MOE, D_MODEL, N_EXPERTS), D_MODEL ** -0.5),
        'moe_w1': nrm(36, (N_MOE, N_EXPERTS, D_MODEL, D_FF), D_MODEL ** -0.5),
        'moe_w3': nrm(37, (N_MOE, N_EXPERTS, D_MODEL, D_FF), D_MODEL ** -0.5),
        'moe_w2': nrm(38, (N_MOE, N_EXPERTS, D_FF, D_MODEL), D_FF ** -0.5),
    }


def reference(x_prompt, x_sample, state_gla, state_hgrn, state_rglru, state_conv, cache_mem_k, cache_mem_v,
              mem_prompt, norm_mix, norm_xattn, norm_ffn, norm_mem, norm_final, w_in, gla_wa2, gla_ba,
              gla_onorm, hg_lb, hg_onorm, rg_conv_w, rg_conv_b, rg_wr, rg_br, rg_wi, rg_bi, rg_lambda, w_out,
              xa_wq, xa_wk, xa_wv, xa_wo, ffd_w1, ffd_w3, ffd_w2, moe_router, moe_w1, moe_w3, moe_w2):
    W = dict(norm_mix=norm_mix, norm_xattn=norm_xattn, norm_ffn=norm_ffn, norm_final=norm_final,
             w_in=w_in, gla_wa2=gla_wa2, gla_ba=gla_ba, gla_onorm=gla_onorm, hg_lb=hg_lb, hg_onorm=hg_onorm,
             rg_conv_w=rg_conv_w, rg_conv_b=rg_conv_b, rg_wr=rg_wr, rg_br=rg_br, rg_wi=rg_wi, rg_bi=rg_bi,
             rg_lambda=rg_lambda, w_out=w_out, xa_wq=xa_wq, xa_wo=xa_wo,
             ffd_w1=ffd_w1, ffd_w3=ffd_w3, ffd_w2=ffd_w2,
             moe_router=moe_router, moe_w1=moe_w1, moe_w3=moe_w3, moe_w2=moe_w2)
    mks, mvs = [], []
    for l in range(DEPTH):
        mk, mv = mem_kv(mem_prompt, norm_mem[l], xa_wk[l], xa_wv[l])
        mks.append(mk)
        mvs.append(mv)
    mem_k_p = jnp.stack(mks)
    mem_v_p = jnp.stack(mvs)
    dt = x_prompt.dtype
    z_gla = jnp.zeros((DEPTH, BATCH, GLA_HEADS, GLA_DK, GLA_DV), dt)
    z_hg = jnp.zeros((DEPTH, BATCH, HG_HEADS, HG_DK, HG_DV), dt)
    z_rg = jnp.zeros((DEPTH, BATCH, RG_WIDTH), dt)
    z_cv = jnp.zeros((DEPTH, BATCH, RG_CONV - 1, RG_WIDTH), dt)
    y_prompt, gla_p, hgrn_p, rglru_p, conv_p = trunk(x_prompt, mem_k_p, mem_v_p, z_gla, z_hg, z_rg, z_cv, W)
    y_sample, gla_s, hgrn_s, rglru_s, conv_s = trunk(x_sample, cache_mem_k, cache_mem_v, state_gla, state_hgrn,
                                                     state_rglru, state_conv, W)
    return (y_prompt, y_sample, gla_p, hgrn_p, rglru_p, conv_p, mem_k_p, mem_v_p, gla_s, hgrn_s, rglru_s, conv_s)
```

```python
import functools

import jax
import jax.numpy as jnp
from jax import lax
from jax.experimental import pallas as pl
from jax.experimental.pallas import tpu as pltpu

F32 = jnp.float32
BF16 = jnp.bfloat16

D_MODEL = 1024
GLA_HEADS = 4
GLA_DK = 48
GLA_DV = 96
GLA_RANK = 16
GLA_TAU = 16.0
HG_HEADS = 4
HG_DK = 64
HG_DV = 96
RG_WIDTH = 256
RG_BLOCKS = 4
RG_CONV = 4
RG_C = 8.0
MEM_LEN = 256
XA_HEADS = 4
XA_DH = D_MODEL // XA_HEADS
D_FF = 2816
N_EXPERTS = 8
EPS = 1e-6
IN_SIZES = (192, 192, 384, 384, 16, 256, 256, 384, 384, 256, 256)

LANE = 128
SUBLANE = 8
VMEM_LIMIT_BYTES = 56 * 1024 * 1024

HEADS = 4
HEAD_K = 64
KL = HEADS * HEAD_K
HEAD_V = 96
VL = HEADS * HEAD_V
SUB = 16
GA_PAD = LANE

OFF_GQ = 0
OFF_GK = OFF_GQ + KL
OFF_GA = OFF_GK + KL
OFF_GV = OFF_GA + GA_PAD
OFF_GR = OFF_GV + VL
OFF_HQ = OFF_GR + VL
OFF_HF = OFF_HQ + KL
OFF_HI = OFF_HF + KL
OFF_HG = OFF_HI + VL
OFF_RX = OFF_HG + VL
OFF_RG = OFF_RX + RG_WIDTH
PROJ_W = OFF_RG + RG_WIDTH


def _cparams(*sem):
    return pltpu.CompilerParams(dimension_semantics=sem, vmem_limit_bytes=VMEM_LIMIT_BYTES)


def _dot(a, b):
    return jnp.dot(a, b, preferred_element_type=F32)


def _dot_nt(a, b):
    return lax.dot_general(a, b, (((1,), (1,)), ((), ())), preferred_element_type=F32)


def _dot_tn(a, b):
    return lax.dot_general(a, b, (((0,), (0,)), ((), ())), preferred_element_type=F32)


def _sigmoid(x):
    return 1.0 / (1.0 + jnp.exp(-x))


def _silu(x):
    return x * _sigmoid(x)


def _softplus(x):
    return jnp.maximum(x, 0.0) + jnp.log1p(jnp.exp(-jnp.abs(x)))


def _expm1(x):
    return jnp.tanh(0.5 * x) * (jnp.exp(x) + 1.0)


def _rms(x, g):
    return x * lax.rsqrt(jnp.mean(x * x, axis=-1, keepdims=True) + EPS) * g


def _split3(x):
    hi = x.astype(BF16)
    r1 = x - hi.astype(F32)
    mid = r1.astype(BF16)
    lo = (r1 - mid.astype(F32)).astype(BF16)
    return hi, mid, lo


def _proj_body(x_ref, g_ref, w_ref, o_ref):
    h = _rms(x_ref[...], g_ref[...]).astype(BF16)
    o_ref[...] = _dot(h, w_ref[...])


def _proj(x, g, w, tm):
    m, d = x.shape
    n = w.shape[1]
    return pl.pallas_call(
        _proj_body,
        grid=(m // tm,),
        in_specs=[
            pl.BlockSpec((tm, d), lambda i: (i, 0)),
            pl.BlockSpec((1, d), lambda i: (0, 0)),
            pl.BlockSpec((d, n), lambda i: (0, 0)),
        ],
        out_specs=pl.BlockSpec((tm, n), lambda i: (i, 0)),
        out_shape=jax.ShapeDtypeStruct((m, n), F32),
        compiler_params=_cparams("parallel"),
        name="in_proj",
    )(x, g, w)


def _memkv_body(m_ref, g_ref, wk_ref, wv_ref, k_ref, v_ref):
    h = _rms(m_ref[...], g_ref[...]).astype(BF16)
    k_ref[...] = _dot(h, wk_ref[...])
    v_ref[...] = _dot(h, wv_ref[...])


def _mem_kv(mem, g, wk, wv, tm):
    m, d = mem.shape
    nl = g.shape[0]
    out = jax.ShapeDtypeStruct((nl, m, d), F32)
    return pl.pallas_call(
        _memkv_body,
        grid=(nl, m // tm),
        in_specs=[
            pl.BlockSpec((tm, d), lambda l, i: (i, 0)),
            pl.BlockSpec((None, 1, d), lambda l, i: (l, 0, 0)),
            pl.BlockSpec((None, d, d), lambda l, i: (l, 0, 0)),
            pl.BlockSpec((None, d, d), lambda l, i: (l, 0, 0)),
        ],
        out_specs=[
            pl.BlockSpec((None, tm, d), lambda l, i: (l, i, 0)),
            pl.BlockSpec((None, tm, d), lambda l, i: (l, i, 0)),
        ],
        out_shape=[out, out],
        compiler_params=_cparams("parallel", "parallel"),
        name="mem_kv",
    )(mem, g, wk, wv)


def _key_head(shape, dim):
    return lax.shift_right_logical(lax.broadcasted_iota(jnp.int32, shape, dim), 6)


def _val_head(shape, dim):
    i = lax.broadcasted_iota(jnp.int32, shape, dim)
    one = jnp.ones(shape, jnp.int32)
    zero = jnp.zeros(shape, jnp.int32)
    return (jnp.where(i >= HEAD_V, one, zero) + jnp.where(i >= 2 * HEAD_V, one, zero)
            + jnp.where(i >= 3 * HEAD_V, one, zero))


def _gated_chunk(q, k, v, lg, st_ref, c):
    nsub = c // SUB
    row = lax.broadcasted_iota(jnp.int32, (c, c), 0)
    col = lax.broadcasted_iota(jnp.int32, (c, c), 1)
    ltri = jnp.where(col <= row, 1.0, 0.0).astype(BF16)
    hi, mid, lo = _split3(lg)
    b = _dot(ltri, hi) + _dot(ltri, mid) + _dot(ltri, lo)
    b_last = b[c - 1:c, :]

    s_t = st_ref[...]
    o_inter = _dot_nt((q * jnp.exp(b)).astype(BF16), s_t.astype(BF16))
    k_end = (k * jnp.exp(b_last - b)).astype(BF16)
    upd = _dot_tn(v.astype(BF16), k_end)
    same_head = _val_head((VL, KL), 0) == _key_head((VL, KL), 1)
    st_ref[...] = s_t * jnp.exp(b_last) + jnp.where(same_head, upd, 0.0)

    khead = _key_head((SUB, KL), 1)
    vhead = _val_head((SUB, VL), 1)
    rr = lax.broadcasted_iota(jnp.int32, (SUB, HEADS * SUB), 0)
    cc = lax.broadcasted_iota(jnp.int32, (SUB, HEADS * SUB), 1)
    causal = jnp.bitwise_and(cc, SUB - 1) <= rr
    pieces = [o_inter[i * SUB:(i + 1) * SUB, :] for i in range(nsub)]
    for j in range(nsub):
        r0 = j * SUB
        e_j = b[r0 + SUB - 1:r0 + SUB, :]
        e_prev = b[r0 - 1:r0, :] if j else jnp.zeros_like(e_j)
        m = 0.5 * (e_prev + e_j)
        lhs = (q[r0:, :] * jnp.exp(b[r0:, :] - m)).astype(BF16)
        ku = k[r0:r0 + SUB, :] * jnp.exp(m - b[r0:r0 + SUB, :])
        ks = jnp.concatenate([jnp.where(khead == h, ku, 0.0) for h in range(HEADS)], axis=0).astype(BF16)
        p = _dot_nt(lhs, ks)
        vj = v[r0:r0 + SUB, :]
        vs = jnp.concatenate([jnp.where(vhead == h, vj, 0.0) for h in range(HEADS)], axis=0).astype(BF16)
        p_diag = jnp.where(causal, p[:SUB, :], 0.0)
        if c - r0 > SUB:
            p = jnp.concatenate([p_diag, p[SUB:, :]], axis=0)
        else:
            p = p_diag
        pv = _dot(p.astype(BF16), vs)
        for i in range(j, nsub):
            pieces[i] = pieces[i] + pv[(i - j) * SUB:(i - j + 1) * SUB, :]
    return jnp.concatenate(pieces, axis=0) if nsub > 1 else pieces[0]


def _head_rms(o, gain):
    seg = (_val_head((VL, VL), 0) == _val_head((VL, VL), 1))
    ones = jnp.where(seg, 1.0, 0.0).astype(BF16)
    sq = o * o
    hi = sq.astype(BF16)
    lo = (sq - hi.astype(F32)).astype(BF16)
    ms = (_dot(hi, ones) + _dot(lo, ones)) * (1.0 / HEAD_V)
    return o * lax.rsqrt(ms + EPS) * gain


def _mixer_body(p_ref, wa2_ref, ba_ref, gon_ref, lb_ref, hon_ref, cw_ref, cb_ref, wr_ref, br_ref, wi_ref,
                bi_ref, lam_ref, sg0_ref, sh0_ref, hr0_ref, cv0_ref,
                mix_ref, sg_ref, sh_ref, hr_ref, cv_ref,
                sg_t, sh_t, h_car, xc, *, c):
    t = pl.program_id(1)
    nt = pl.num_programs(1)
    npad = SUBLANE
    ncar = RG_CONV - 1

    @pl.when(t == 0)
    def _():
        sg_t[...] = sg0_ref[...]
        sh_t[...] = sh0_ref[...]
        h_car[...] = hr0_ref[...]
        xc[npad - ncar:npad, :] = cv0_ref[...]

    gq = p_ref[:, OFF_GQ:OFF_GQ + KL] * (GLA_DK ** -0.5)
    gk = p_ref[:, OFF_GK:OFF_GK + KL]
    ga = p_ref[:, OFF_GA:OFF_GA + GA_PAD]
    z = _dot(ga.astype(BF16), wa2_ref[...]) + ba_ref[...]
    lg = -_softplus(-z) * (1.0 / GLA_TAU)
    o = _gated_chunk(gq, gk, p_ref[:, OFF_GV:OFF_GV + VL], lg, sg_t, c)
    o = _head_rms(o, gon_ref[...]) * _silu(p_ref[:, OFF_GR:OFF_GR + VL])
    mix_ref[:, 0:VL] = o.astype(BF16)

    z = p_ref[:, OFF_HF:OFF_HF + KL]
    lb = lb_ref[...]
    f = lb + (1.0 - lb) * _sigmoid(z)
    hk = (1.0 - lb) * _sigmoid(-z)
    hq = _silu(p_ref[:, OFF_HQ:OFF_HQ + KL])
    o = _gated_chunk(hq, hk, p_ref[:, OFF_HI:OFF_HI + VL], jnp.log(f), sh_t, c)
    o = _head_rms(o, hon_ref[...]) * _silu(p_ref[:, OFF_HG:OFF_HG + VL])
    mix_ref[:, VL:2 * VL] = o.astype(BF16)

    xb = p_ref[:, OFF_RX:OFF_RX + RG_WIDTH]
    xc[npad:npad + c, :] = xb
    u = cb_ref[...]
    for j in range(RG_CONV):
        u = u + xc[npad - ncar + j:npad - ncar + j + c, :] * cw_ref[j:j + 1, :]
    xc[npad - ncar:npad, :] = xc[npad + c - ncar:npad + c, :]
    ub = u.astype(BF16)
    r = _sigmoid(_dot(ub, wr_ref[...]) + br_ref[...])
    ig = _sigmoid(_dot(ub, wi_ref[...]) + bi_ref[...])
    log_a = (-RG_C * r) * _softplus(-lam_ref[...])
    a = jnp.exp(log_a)
    beta = jnp.sqrt(jnp.maximum(-_expm1(2.0 * log_a), 1e-12))
    bt = beta * (ig * u)
    rows = lax.broadcasted_iota(jnp.int32, (c, RG_WIDTH), 0)
    d = 1
    while d < c:
        keep = rows >= d
        a_sh = jnp.where(keep, pltpu.roll(a, d, 0), 1.0)
        b_sh = jnp.where(keep, pltpu.roll(bt, d, 0), 0.0)
        bt = a * b_sh + bt
        a = a * a_sh
        d *= 2
    h = bt + a * h_car[...]
    h_car[...] = h[c - 1:c, :]
    gb = p_ref[:, OFF_RG:OFF_RG + RG_WIDTH]
    gelu = gb * (0.5 * (1.0 + jnp.tanh(0.7978845608028654 * (gb + 0.044715 * (gb * gb * gb)))))
    mix_ref[:, 2 * VL:2 * VL + RG_WIDTH] = (h * gelu).astype(BF16)

    @pl.when(t == nt - 1)
    def _():
        sg_ref[...] = sg_t[...]
        sh_ref[...] = sh_t[...]
        hr_ref[...] = h_car[...]
        cv_ref[...] = xc[npad - ncar:npad, :]


def _mixer(p, lw, sg0, sh0, hr0, cv0, batch, seq, c):
    nt = seq // c
    vec = lambda n: pl.BlockSpec((1, n), lambda b, t: (0, 0))
    mat = lambda r, n: pl.BlockSpec((r, n), lambda b, t: (0, 0))
    st = lambda r, n: pl.BlockSpec((None, r, n), lambda b, t: (b, 0, 0))
    in_specs = [
        pl.BlockSpec((c, PROJ_W), lambda b, t: (b * nt + t, 0)),
        mat(GA_PAD, KL), vec(KL), vec(VL), vec(KL), vec(VL), mat(SUBLANE, RG_WIDTH), vec(RG_WIDTH),
        mat(RG_WIDTH, RG_WIDTH), vec(RG_WIDTH), mat(RG_WIDTH, RG_WIDTH), vec(RG_WIDTH), vec(RG_WIDTH),
        st(VL, KL), st(VL, KL), st(1, RG_WIDTH), st(RG_CONV - 1, RG_WIDTH),
    ]
    out_specs = [
        pl.BlockSpec((c, D_MODEL), lambda b, t: (b * nt + t, 0)),
        st(VL, KL), st(VL, KL), st(1, RG_WIDTH), st(RG_CONV - 1, RG_WIDTH),
    ]
    out_shape = [
        jax.ShapeDtypeStruct((batch * seq, D_MODEL), BF16),
        jax.ShapeDtypeStruct((batch, VL, KL), F32),
        jax.ShapeDtypeStruct((batch, VL, KL), F32),
        jax.ShapeDtypeStruct((batch, 1, RG_WIDTH), F32),
        jax.ShapeDtypeStruct((batch, RG_CONV - 1, RG_WIDTH), F32),
    ]
    scratch = [
        pltpu.VMEM((VL, KL), F32), pltpu.VMEM((VL, KL), F32), pltpu.VMEM((1, RG_WIDTH), F32),
        pltpu.VMEM((c + SUBLANE, RG_WIDTH), F32),
    ]
    return pl.pallas_call(
        functools.partial(_mixer_body, c=c),
        grid=(batch, nt),
        in_specs=in_specs,
        out_specs=out_specs,
        out_shape=out_shape,
        scratch_shapes=scratch,
        compiler_params=_cparams("parallel", "arbitrary"),
        name="mixer",
    )(p, lw["wa2"], lw["ba"], lw["gla_on"], lw["lb"], lw["hg_on"], lw["conv_w"], lw["conv_b"], lw["wr"],
      lw["br"], lw["wi"], lw["bi"], lw["lam"], sg0, sh0, hr0, cv0)


def _attn_body(*refs, with_router):
    if with_router:
        (x_ref, mix_ref, wout_ref, gx_ref, wq_ref, k_ref, v_ref, wo_ref, gf_ref, rhi_ref, rlo_ref,
         x2_ref, hb_ref, gate_ref) = refs
    else:
        x_ref, mix_ref, wout_ref, gx_ref, wq_ref, k_ref, v_ref, wo_ref, gf_ref, x2_ref, hb_ref = refs
    x1 = x_ref[...] + _dot(mix_ref[...], wout_ref[...])
    h = _rms(x1, gx_ref[...]).astype(BF16)
    q = _dot(h, wq_ref[...])
    outs = []
    for hd in range(XA_HEADS):
        sl = slice(hd * XA_DH, (hd + 1) * XA_DH)
        s = _dot_nt(q[:, sl].astype(BF16), k_ref[:, sl].astype(BF16)) * (XA_DH ** -0.5)
        e = jnp.exp(s - jnp.max(s, axis=-1, keepdims=True))
        p = e / jnp.sum(e, axis=-1, keepdims=True)
        outs.append(_dot(p.astype(BF16), v_ref[:, sl].astype(BF16)))
    o = jnp.concatenate(outs, axis=-1).astype(BF16)
    x2 = x1 + _dot(o, wo_ref[...])
    x2_ref[...] = x2
    h2 = _rms(x2, gf_ref[...])
    h2_hi = h2.astype(BF16)
    hb_ref[...] = h2_hi
    if with_router:
        h2_lo = (h2 - h2_hi.astype(F32)).astype(BF16)
        logits = _dot(h2_hi, rhi_ref[...]) + _dot(h2_lo, rhi_ref[...]) + _dot(h2_hi, rlo_ref[...])
        lane = lax.broadcasted_iota(jnp.int32, logits.shape, 1).astype(F32)
        neg = -jnp.inf
        lg = jnp.where(lane < N_EXPERTS, logits, neg)
        m1 = jnp.max(lg, axis=-1, keepdims=True)
        i1 = jnp.min(jnp.where(lg == m1, lane, float(LANE)), axis=-1, keepdims=True)
        lg2 = jnp.where(lane == i1, neg, lg)
        m2 = jnp.max(lg2, axis=-1, keepdims=True)
        i2 = jnp.min(jnp.where(lg2 == m2, lane, float(LANE)), axis=-1, keepdims=True)
        e2 = jnp.exp(m2 - m1)
        g1 = 1.0 / (1.0 + e2)
        g2 = e2 / (1.0 + e2)
        gate_ref[...] = jnp.where(lane == i1, g1, jnp.where(lane == i2, g2, 0.0))


def _attn(x, mix, mk, mv, lw, g_ffn, router, batch, seq, tq):
    m = batch * seq
    nt = seq // tq
    with_router = router is not None
    row = pl.BlockSpec((tq, D_MODEL), lambda b, t: (b * nt + t, 0))
    vec = pl.BlockSpec((1, D_MODEL), lambda b, t: (0, 0))
    mat = pl.BlockSpec((D_MODEL, D_MODEL), lambda b, t: (0, 0))
    kv = pl.BlockSpec((None, MEM_LEN, D_MODEL), lambda b, t: (b, 0, 0))
    in_specs = [row, row, mat, vec, mat, kv, kv, mat, vec]
    args = [x, mix, lw["w_out"], lw["g_xattn"], lw["wq"], mk, mv, lw["wo"], g_ffn]
    out_specs = [row, row]
    out_shape = [jax.ShapeDtypeStruct((m, D_MODEL), F32), jax.ShapeDtypeStruct((m, D_MODEL), BF16)]
    if with_router:
        rspec = pl.BlockSpec((D_MODEL, LANE), lambda b, t: (0, 0))
        in_specs += [rspec, rspec]
        args += [router[0], router[1]]
        out_specs.append(pl.BlockSpec((tq, LANE), lambda b, t: (b * nt + t, 0)))
        out_shape.append(jax.ShapeDtypeStruct((m, LANE), F32))
    return pl.pallas_call(
        functools.partial(_attn_body, with_router=with_router),
        grid=(batch, nt),
        in_specs=in_specs,
        out_specs=out_specs,
        out_shape=out_shape,
        compiler_params=_cparams("parallel", "parallel"),
        name="xattn",
    )(*args)


FF_CHUNK = 256


def _swiglu_tile(h, w1_ref, w3_ref, w2_ref):
    y = None
    for f0 in range(0, D_FF, FF_CHUNK):
        a = _dot(h, w1_ref[:, f0:f0 + FF_CHUNK])
        c = _dot(h, w3_ref[:, f0:f0 + FF_CHUNK])
        g = (_silu(a) * c).astype(BF16)
        part = _dot(g, w2_ref[f0:f0 + FF_CHUNK, :])
        y = part if y is None else y + part
    return y


def _ffn_body(*refs, n_e, final_norm):
    if n_e > 1:
        hb_ref, x_ref, gate_ref, w1_ref, w3_ref, w2_ref, gfin_ref, o_ref, acc_ref = refs
    else:
        hb_ref, x_ref, w1_ref, w3_ref, w2_ref, gfin_ref, o_ref, acc_ref = refs
    e = pl.program_id(1)

    @pl.when(e == 0)
    def _():
        acc_ref[...] = x_ref[...]

    y = _swiglu_tile(hb_ref[...], w1_ref, w3_ref, w2_ref)
    if n_e > 1:
        gates = gate_ref[...]
        lane = lax.broadcasted_iota(jnp.int32, gates.shape, 1)
        y = jnp.sum(jnp.where(lane == e, gates, 0.0), axis=-1, keepdims=True) * y
    acc_ref[...] += y

    @pl.when(e == n_e - 1)
    def _():
        out = acc_ref[...]
        o_ref[...] = _rms(out, gfin_ref[...]) if final_norm else out


def _ffn(hb, x, gates, w1, w3, w2, g_final, tm, final_norm):
    m = x.shape[0]
    n_e = w1.shape[0]
    row = lambda n: pl.BlockSpec((tm, n), lambda i, e: (i, 0))
    in_specs = [row(D_MODEL), row(D_MODEL)]
    args = [hb, x]
    if n_e > 1:
        in_specs.append(row(LANE))
        args.append(gates)
    in_specs += [
        pl.BlockSpec((None, D_MODEL, D_FF), lambda i, e: (e, 0, 0)),
        pl.BlockSpec((None, D_MODEL, D_FF), lambda i, e: (e, 0, 0)),
        pl.BlockSpec((None, D_FF, D_MODEL), lambda i, e: (e, 0, 0)),
        pl.BlockSpec((1, D_MODEL), lambda i, e: (0, 0)),
    ]
    args += [w1, w3, w2, g_final]
    return pl.pallas_call(
        functools.partial(_ffn_body, n_e=n_e, final_norm=final_norm),
        grid=(m // tm, n_e),
        in_specs=in_specs,
        out_specs=row(D_MODEL),
        out_shape=jax.ShapeDtypeStruct((m, D_MODEL), F32),
        scratch_shapes=[pltpu.VMEM((tm, D_MODEL), F32)],
        compiler_params=_cparams("parallel", "arbitrary"),
        name="ffn",
    )(*args)


def _pad_heads(w, dk):
    lead = w.shape[:-1]
    w = w.reshape(lead + (HEADS, dk))
    w = jnp.pad(w, [(0, 0)] * len(lead) + [(0, 0), (0, HEAD_K - dk)])
    return w.reshape(lead + (KL,))


def _block_diag(w):
    nb, n, _ = w.shape
    eye = jnp.eye(nb, dtype=w.dtype)
    return jnp.einsum("ncd,nm->ncmd", w, eye).reshape(nb * n, nb * n)


def _layer_weights(l, P):
    offs = [0]
    for s in IN_SIZES:
        offs.append(offs[-1] + s)
    cols = [P["w_in"][l][:, offs[i]:offs[i + 1]] for i in range(len(IN_SIZES))]
    gq, gk, gv, gr, ga, hq, hf, hi, hg, rx, rg = cols
    w_in = jnp.concatenate([
        _pad_heads(gq, GLA_DK), _pad_heads(gk, GLA_DK), jnp.pad(ga, ((0, 0), (0, GA_PAD - GLA_RANK))), gv, gr,
        hq, hf, hi, hg, rx, rg], axis=1).astype(BF16)
    row = lambda v: v.reshape(1, -1).astype(F32)
    return dict(
        g_mix=row(P["norm_mix"][l]), w_in=w_in,
        wa2=jnp.pad(_pad_heads(P["gla_wa2"][l], GLA_DK), ((0, GA_PAD - GLA_RANK), (0, 0))).astype(BF16),
        ba=row(_pad_heads(P["gla_ba"][l], GLA_DK)), gla_on=row(P["gla_onorm"][l]),
        lb=row(P["lb_all"][l]), hg_on=row(P["hg_onorm"][l]),
        conv_w=jnp.pad(P["rg_conv_w"][l], ((0, SUBLANE - RG_CONV), (0, 0))).astype(F32),
        conv_b=row(P["rg_conv_b"][l]),
        wr=_block_diag(P["rg_wr"][l]).astype(BF16), br=row(P["rg_br"][l]),
        wi=_block_diag(P["rg_wi"][l]).astype(BF16), bi=row(P["rg_bi"][l]),
        lam=row(P["rg_lambda"][l]),
        w_out=P["w_out"][l].astype(BF16), g_xattn=row(P["norm_xattn"][l]),
        wq=P["xa_wq"][l].astype(BF16), wo=P["xa_wo"][l].astype(BF16),
        g_ffn=row(P["norm_ffn"][l]),
    )


def _state_to_kernel(s, dk):
    s = jnp.pad(s, ((0, 0), (0, 0), (0, HEAD_K - dk), (0, 0)))
    eye = jnp.eye(HEADS, dtype=s.dtype)
    b = s.shape[0]
    return jnp.einsum("bhkv,hg->bhvgk", s, eye).reshape(b, VL, KL)


def _state_from_kernel(st, dk):
    b = st.shape[0]
    s = st.reshape(b, HEADS, HEAD_V, HEADS, HEAD_K)
    s = jnp.stack([s[:, h, :, h, :] for h in range(HEADS)], axis=1)
    return jnp.swapaxes(s, -1, -2)[:, :, :dk, :]


def _trunk(x, mem_k, mem_v, s_gla, s_hg, s_rg, s_conv, P, LW):
    batch, seq, _ = x.shape
    depth = len(LW)
    m = batch * seq
    c = min(seq, 128)
    tm = min(m, 512)
    tq = min(seq, 512)
    xf = x.reshape(m, D_MODEL)
    n_gla, n_hg, n_rg, n_cv = [], [], [], []
    for l in range(depth):
        lw = LW[l]
        p = _proj(xf, lw["g_mix"], lw["w_in"], tm)
        mix, sg, sh, hr, cv = _mixer(
            p, lw, _state_to_kernel(s_gla[l], GLA_DK), _state_to_kernel(s_hg[l], HG_DK),
            s_rg[l].reshape(batch, 1, RG_WIDTH), s_conv[l], batch, seq, c)
        moe = l % 2 == 1
        j = l // 2
        router = P["router"][j] if moe else None
        res = _attn(xf, mix, mem_k[l].reshape(batch, MEM_LEN, D_MODEL), mem_v[l].reshape(batch, MEM_LEN, D_MODEL),
                    lw, lw["g_ffn"], router, batch, seq, tq)
        last = l == depth - 1
        if moe:
            x2, hb, gates = res
            xf = _ffn(hb, x2, gates, P["moe_w1"][j], P["moe_w3"][j], P["moe_w2"][j], P["g_final"], tm, last)
        else:
            x2, hb = res
            xf = _ffn(hb, x2, None, P["ffd_w1"][j:j + 1], P["ffd_w3"][j:j + 1], P["ffd_w2"][j:j + 1],
                      P["g_final"], tm, last)
        n_gla.append(_state_from_kernel(sg, GLA_DK))
        n_hg.append(_state_from_kernel(sh, HG_DK))
        n_rg.append(hr.reshape(batch, RG_WIDTH))
        n_cv.append(cv)
    return (xf.reshape(batch, seq, D_MODEL), jnp.stack(n_gla), jnp.stack(n_hg), jnp.stack(n_rg), jnp.stack(n_cv))


def kernel(x_prompt, x_sample, state_gla, state_hgrn, state_rglru, state_conv, cache_mem_k, cache_mem_v, mem_prompt, norm_mix, norm_xattn, norm_ffn, norm_mem, norm_final, w_in, gla_wa2, gla_ba, gla_onorm, hg_lb, hg_onorm, rg_conv_w, rg_conv_b, rg_wr, rg_br, rg_wi, rg_bi, rg_lambda, w_out, xa_wq, xa_wk, xa_wv, xa_wo, ffd_w1, ffd_w3, ffd_w2, moe_router, moe_w1, moe_w3, moe_w2):
    depth = w_in.shape[0]
    batch = x_prompt.shape[0]
    pl_ = jax.nn.softmax(hg_lb.astype(F32), axis=0)
    lb_all = jnp.cumsum(pl_, axis=0) - pl_[:1]
    r_hi = jnp.pad(moe_router, ((0, 0), (0, 0), (0, LANE - N_EXPERTS)))
    r_hi_b = r_hi.astype(BF16)
    r_lo_b = (r_hi - r_hi_b.astype(F32)).astype(BF16)
    P = dict(
        w_in=w_in, gla_wa2=gla_wa2, gla_ba=gla_ba, gla_onorm=gla_onorm, lb_all=lb_all, hg_onorm=hg_onorm,
        rg_conv_w=rg_conv_w, rg_conv_b=rg_conv_b, rg_wr=rg_wr, rg_br=rg_br, rg_wi=rg_wi, rg_bi=rg_bi,
        rg_lambda=rg_lambda, w_out=w_out, xa_wq=xa_wq, xa_wo=xa_wo, norm_mix=norm_mix, norm_xattn=norm_xattn,
        norm_ffn=norm_ffn,
        router=[(r_hi_b[j], r_lo_b[j]) for j in range(moe_router.shape[0])],
        ffd_w1=ffd_w1.astype(BF16), ffd_w3=ffd_w3.astype(BF16), ffd_w2=ffd_w2.astype(BF16),
        moe_w1=moe_w1.astype(BF16), moe_w3=moe_w3.astype(BF16), moe_w2=moe_w2.astype(BF16),
        g_final=norm_final.reshape(1, D_MODEL).astype(F32),
    )
    LW = [_layer_weights(l, P) for l in range(depth)]

    mem_flat = mem_prompt.reshape(batch * MEM_LEN, D_MODEL)
    mk, mv = _mem_kv(mem_flat, norm_mem.reshape(depth, 1, D_MODEL).astype(F32), xa_wk.astype(BF16),
                     xa_wv.astype(BF16), 512)
    mem_k_p = mk.reshape(depth, batch, MEM_LEN, XA_HEADS, XA_DH)
    mem_v_p = mv.reshape(depth, batch, MEM_LEN, XA_HEADS, XA_DH)
    dt = x_prompt.dtype
    z_gla = jnp.zeros((depth, batch, GLA_HEADS, GLA_DK, GLA_DV), dt)
    z_hg = jnp.zeros((depth, batch, HG_HEADS, HG_DK, HG_DV), dt)
    z_rg = jnp.zeros((depth, batch, RG_WIDTH), dt)
    z_cv = jnp.zeros((depth, batch, RG_CONV - 1, RG_WIDTH), dt)
    y_prompt, gla_p, hgrn_p, rglru_p, conv_p = _trunk(x_prompt, mem_k_p, mem_v_p, z_gla, z_hg, z_rg, z_cv, P, LW)
    y_sample, gla_s, hgrn_s, rglru_s, conv_s = _trunk(x_sample, cache_mem_k, cache_mem_v, state_gla, state_hgrn,
                                                      state_rglru, state_conv, P, LW)
    return (y_prompt, y_sample, gla_p, hgrn_p, rglru_p, conv_p, mem_k_p, mem_v_p, gla_s, hgrn_s, rglru_s, conv_s)
```

```python
import functools

import jax
import jax.numpy as jnp
from jax import lax
from jax.experimental import pallas as pl
from jax.experimental.pallas import tpu as pltpu

F32 = jnp.float32
BF16 = jnp.bfloat16

D_MODEL = 1024
GLA_HEADS = 4
GLA_DK = 48
GLA_DV = 96
GLA_RANK = 16
GLA_TAU = 16.0
HG_HEADS = 4
HG_DK = 64
HG_DV = 96
RG_WIDTH = 256
RG_BLOCKS = 4
RG_CONV = 4
RG_C = 8.0
MEM_LEN = 256
XA_HEADS = 4
XA_DH = D_MODEL // XA_HEADS
D_FF = 2816
N_EXPERTS = 8
EPS = 1e-6
IN_SIZES = (192, 192, 384, 384, 16, 256, 256, 384, 384, 256, 256)

LANE = 128
SUBLANE = 8
VMEM_LIMIT_BYTES = 56 * 1024 * 1024

HEADS = 4
HEAD_K = 64
KL = HEADS * HEAD_K
HEAD_V = 96
VL = HEADS * HEAD_V
SUB = 16
GA_PAD = LANE

OFF_GQ = 0
OFF_GK = OFF_GQ + KL
OFF_GA = OFF_GK + KL
OFF_GV = OFF_GA + GA_PAD
OFF_GR = OFF_GV + VL
OFF_HQ = OFF_GR + VL
OFF_HF = OFF_HQ + KL
OFF_HI = OFF_HF + KL
OFF_HG = OFF_HI + VL
OFF_RX = OFF_HG + VL
OFF_RG = OFF_RX + RG_WIDTH
PROJ_W = OFF_RG + RG_WIDTH


def _cparams(*sem):
    return pltpu.CompilerParams(dimension_semantics=sem, vmem_limit_bytes=VMEM_LIMIT_BYTES)


def _dot(a, b):
    return jnp.dot(a, b, preferred_element_type=F32)


def _dot_nt(a, b):
    return lax.dot_general(a, b, (((1,), (1,)), ((), ())), preferred_element_type=F32)


def _dot_tn(a, b):
    return lax.dot_general(a, b, (((0,), (0,)), ((), ())), preferred_element_type=F32)


def _sigmoid(x):
    return 1.0 / (1.0 + jnp.exp(-x))


def _silu(x):
    return x * _sigmoid(x)


def _softplus(x):
    return jnp.maximum(x, 0.0) + jnp.log1p(jnp.exp(-jnp.abs(x)))


def _expm1(x):
    return jnp.tanh(0.5 * x) * (jnp.exp(x) + 1.0)


def _rms(x, g):
    return x * lax.rsqrt(jnp.mean(x * x, axis=-1, keepdims=True) + EPS) * g


def _split3(x):
    hi = x.astype(BF16)
    r1 = x - hi.astype(F32)
    mid = r1.astype(BF16)
    lo = (r1 - mid.astype(F32)).astype(BF16)
    return hi, mid, lo


def _proj_body(x_ref, g_ref, w_ref, o_ref):
    h = _rms(x_ref[...], g_ref[...]).astype(BF16)
    o_ref[...] = _dot(h, w_ref[...])


def _proj(x, g, w, tm):
    m, d = x.shape
    n = w.shape[1]
    return pl.pallas_call(
        _proj_body,
        grid=(m // tm,),
        in_specs=[
            pl.BlockSpec((tm, d), lambda i: (i, 0)),
            pl.BlockSpec((1, d), lambda i: (0, 0)),
            pl.BlockSpec((d, n), lambda i: (0, 0)),
        ],
        out_specs=pl.BlockSpec((tm, n), lambda i: (i, 0)),
        out_shape=jax.ShapeDtypeStruct((m, n), F32),
        compiler_params=_cparams("parallel"),
        name="in_proj",
    )(x, g, w)


def _memkv_body(m_ref, g_ref, wk_ref, wv_ref, k_ref, v_ref):
    h = _rms(m_ref[...], g_ref[...]).astype(BF16)
    k_ref[...] = _dot(h, wk_ref[...])
    v_ref[...] = _dot(h, wv_ref[...])


def _mem_kv(mem, g, wk, wv, tm):
    m, d = mem.shape
    nl = g.shape[0]
    out = jax.ShapeDtypeStruct((nl, m, d), F32)
    return pl.pallas_call(
        _memkv_body,
        grid=(nl, m // tm),
        in_specs=[
            pl.BlockSpec((tm, d), lambda l, i: (i, 0)),
            pl.BlockSpec((None, 1, d), lambda l, i: (l, 0, 0)),
            pl.BlockSpec((None, d, d), lambda l, i: (l, 0, 0)),
            pl.BlockSpec((None, d, d), lambda l, i: (l, 0, 0)),
        ],
        out_specs=[
            pl.BlockSpec((None, tm, d), lambda l, i: (l, i, 0)),
            pl.BlockSpec((None, tm, d), lambda l, i: (l, i, 0)),
        ],
        out_shape=[out, out],
        compiler_params=_cparams("parallel", "parallel"),
        name="mem_kv",
    )(mem, g, wk, wv)


def _key_head(shape, dim):
    return lax.shift_right_logical(lax.broadcasted_iota(jnp.int32, shape, dim), 6)


def _val_head(shape, dim):
    i = lax.broadcasted_iota(jnp.int32, shape, dim)
    one = jnp.ones(shape, jnp.int32)
    zero = jnp.zeros(shape, jnp.int32)
    return (jnp.where(i >= HEAD_V, one, zero) + jnp.where(i >= 2 * HEAD_V, one, zero)
            + jnp.where(i >= 3 * HEAD_V, one, zero))


def _gated_chunk(q, k, v, lg, st_ref, c):
    nsub = c // SUB
    row = lax.broadcasted_iota(jnp.int32, (c, c), 0)
    col = lax.broadcasted_iota(jnp.int32, (c, c), 1)
    ltri = jnp.where(col <= row, 1.0, 0.0).astype(BF16)
    hi, mid, lo = _split3(lg)
    b = _dot(ltri, hi) + _dot(ltri, mid) + _dot(ltri, lo)
    b_last = b[c - 1:c, :]

    s_t = st_ref[...]
    o_inter = _dot_nt((q * jnp.exp(b)).astype(BF16), s_t.astype(BF16))
    k_end = (k * jnp.exp(b_last - b)).astype(BF16)
    upd = _dot_tn(v.astype(BF16), k_end)
    same_head = _val_head((VL, KL), 0) == _key_head((VL, KL), 1)
    st_ref[...] = s_t * jnp.exp(b_last) + jnp.where(same_head, upd, 0.0)

    khead = _key_head((SUB, KL), 1)
    vhead = _val_head((SUB, VL), 1)
    rr = lax.broadcasted_iota(jnp.int32, (SUB, HEADS * SUB), 0)
    cc = lax.broadcasted_iota(jnp.int32, (SUB, HEADS * SUB), 1)
    causal = jnp.bitwise_and(cc, SUB - 1) <= rr
    pieces = [o_inter[i * SUB:(i + 1) * SUB, :] for i in range(nsub)]
    for j in range(nsub):
        r0 = j * SUB
        e_j = b[r0 + SUB - 1:r0 + SUB, :]
        e_prev = b[r0 - 1:r0, :] if j else jnp.zeros_like(e_j)
        m = 0.5 * (e_prev + e_j)
        lhs = (q[r0:, :] * jnp.exp(b[r0:, :] - m)).astype(BF16)
        ku = k[r0:r0 + SUB, :] * jnp.exp(m - b[r0:r0 + SUB, :])
        ks = jnp.concatenate([jnp.where(khead == h, ku, 0.0) for h in range(HEADS)], axis=0).astype(BF16)
        p = _dot_nt(lhs, ks)
        vj = v[r0:r0 + SUB, :]
        vs = jnp.concatenate([jnp.where(vhead == h, vj, 0.0) for h in range(HEADS)], axis=0).astype(BF16)
        p_diag = jnp.where(causal, p[:SUB, :], 0.0)
        if c - r0 > SUB:
            p = jnp.concatenate([p_diag, p[SUB:, :]], axis=0)
        else:
            p = p_diag
        pv = _dot(p.astype(BF16), vs)
        for i in range(j, nsub):
            pieces[i] = pieces[i] + pv[(i - j) * SUB:(i - j + 1) * SUB, :]
    return jnp.concatenate(pieces, axis=0) if nsub > 1 else pieces[0]


def _head_rms(o, gain):
    seg = (_val_head((VL, VL), 0) == _val_head((VL, VL), 1))
    ones = jnp.where(seg, 1.0, 0.0).astype(BF16)
    sq = o * o
    hi = sq.astype(BF16)
    lo = (sq - hi.astype(F32)).astype(BF16)
    ms = (_dot(hi, ones) + _dot(lo, ones)) * (1.0 / HEAD_V)
    return o * lax.rsqrt(ms + EPS) * gain


def _mixer_body(p_ref, wa2_ref, ba_ref, gon_ref, lb_ref, hon_ref, cw_ref, cb_ref, wr_ref, br_ref, wi_ref,
                bi_ref, lam_ref, sg0_ref, sh0_ref, hr0_ref, cv0_ref,
                mix_ref, sg_ref, sh_ref, hr_ref, cv_ref,
                sg_t, sh_t, h_car, xc, *, c):
    t = pl.program_id(1)
    nt = pl.num_programs(1)
    npad = SUBLANE
    ncar = RG_CONV - 1

    @pl.when(t == 0)
    def _():
        sg_t[...] = sg0_ref[...]
        sh_t[...] = sh0_ref[...]
        h_car[...] = hr0_ref[...]
        xc[npad - ncar:npad, :] = cv0_ref[...]

    gq = p_ref[:, OFF_GQ:OFF_GQ + KL] * (GLA_DK ** -0.5)
    gk = p_ref[:, OFF_GK:OFF_GK + KL]
    ga = p_ref[:, OFF_GA:OFF_GA + GA_PAD]
    z = _dot(ga.astype(BF16), wa2_ref[...]) + ba_ref[...]
    lg = -_softplus(-z) * (1.0 / GLA_TAU)
    o = _gated_chunk(gq, gk, p_ref[:, OFF_GV:OFF_GV + VL], lg, sg_t, c)
    o = _head_rms(o, gon_ref[...]) * _silu(p_ref[:, OFF_GR:OFF_GR + VL])
    mix_ref[:, 0:VL] = o.astype(BF16)

    z = p_ref[:, OFF_HF:OFF_HF + KL]
    lb = lb_ref[...]
    f = lb + (1.0 - lb) * _sigmoid(z)
    hk = (1.0 - lb) * _sigmoid(-z)
    hq = _silu(p_ref[:, OFF_HQ:OFF_HQ + KL])
    o = _gated_chunk(hq, hk, p_ref[:, OFF_HI:OFF_HI + VL], jnp.log(f), sh_t, c)
    o = _head_rms(o, hon_ref[...]) * _silu(p_ref[:, OFF_HG:OFF_HG + VL])
    mix_ref[:, VL:2 * VL] = o.astype(BF16)

    xb = p_ref[:, OFF_RX:OFF_RX + RG_WIDTH]
    xc[npad:npad + c, :] = xb
    u = cb_ref[...]
    for j in range(RG_CONV):
        u = u + xc[npad - ncar + j:npad - ncar + j + c, :] * cw_ref[j:j + 1, :]
    xc[npad - ncar:npad, :] = xc[npad + c - ncar:npad + c, :]
    ub = u.astype(BF16)
    r = _sigmoid(_dot(ub, wr_ref[...]) + br_ref[...])
    ig = _sigmoid(_dot(ub, wi_ref[...]) + bi_ref[...])
    log_a = (-RG_C * r) * _softplus(-lam_ref[...])
    a = jnp.exp(log_a)
    beta = jnp.sqrt(jnp.maximum(-_expm1(2.0 * log_a), 1e-12))
    bt = beta * (ig * u)
    rows = lax.broadcasted_iota(jnp.int32, (c, RG_WIDTH), 0)
    d = 1
    while d < c:
        keep = rows >= d
        a_sh = jnp.where(keep, pltpu.roll(a, d, 0), 1.0)
        b_sh = jnp.where(keep, pltpu.roll(bt, d, 0), 0.0)
        bt = a * b_sh + bt
        a = a * a_sh
        d *= 2
    h = bt + a * h_car[...]
    h_car[...] = h[c - 1:c, :]
    gb = p_ref[:, OFF_RG:OFF_RG + RG_WIDTH]
    gelu = gb * (0.5 * (1.0 + jnp.tanh(0.7978845608028654 * (gb + 0.044715 * (gb * gb * gb)))))
    mix_ref[:, 2 * VL:2 * VL + RG_WIDTH] = (h * gelu).astype(BF16)

    @pl.when(t == nt - 1)
    def _():
        sg_ref[...] = sg_t[...]
        sh_ref[...] = sh_t[...]
        hr_ref[...] = h_car[...]
        cv_ref[...] = xc[npad - ncar:npad, :]


def _mixer(p, lw, sg0, sh0, hr0, cv0, batch, seq, c):
    nt = seq // c
    vec = lambda n: pl.BlockSpec((1, n), lambda b, t: (0, 0))
    mat = lambda r, n: pl.BlockSpec((r, n), lambda b, t: (0, 0))
    st = lambda r, n: pl.BlockSpec((None, r, n), lambda b, t: (b, 0, 0))
    in_specs = [
        pl.BlockSpec((c, PROJ_W), lambda b, t: (b * nt + t, 0)),
        mat(GA_PAD, KL), vec(KL), vec(VL), vec(KL), vec(VL), mat(SUBLANE, RG_WIDTH), vec(RG_WIDTH),
        mat(RG_WIDTH, RG_WIDTH), vec(RG_WIDTH), mat(RG_WIDTH, RG_WIDTH), vec(RG_WIDTH), vec(RG_WIDTH),
        st(VL, KL), st(VL, KL), st(1, RG_WIDTH), st(RG_CONV - 1, RG_WIDTH),
    ]
    out_specs = [
        pl.BlockSpec((c, D_MODEL), lambda b, t: (b * nt + t, 0)),
        st(VL, KL), st(VL, KL), st(1, RG_WIDTH), st(RG_CONV - 1, RG_WIDTH),
    ]
    out_shape = [
        jax.ShapeDtypeStruct((batch * seq, D_MODEL), BF16),
        jax.ShapeDtypeStruct((batch, VL, KL), F32),
        jax.ShapeDtypeStruct((batch, VL, KL), F32),
        jax.ShapeDtypeStruct((batch, 1, RG_WIDTH), F32),
        jax.ShapeDtypeStruct((batch, RG_CONV - 1, RG_WIDTH), F32),
    ]
    scratch = [
        pltpu.VMEM((VL, KL), F32), pltpu.VMEM((VL, KL), F32), pltpu.VMEM((1, RG_WIDTH), F32),
        pltpu.VMEM((c + SUBLANE, RG_WIDTH), F32),
    ]
    return pl.pallas_call(
        functools.partial(_mixer_body, c=c),
        grid=(batch, nt),
        in_specs=in_specs,
        out_specs=out_specs,
        out_shape=out_shape,
        scratch_shapes=scratch,
        compiler_params=_cparams("parallel", "arbitrary"),
        name="mixer",
    )(p, lw["wa2"], lw["ba"], lw["gla_on"], lw["lb"], lw["hg_on"], lw["conv_w"], lw["conv_b"], lw["wr"],
      lw["br"], lw["wi"], lw["bi"], lw["lam"], sg0, sh0, hr0, cv0)


ROUTE_I1, ROUTE_I2, ROUTE_G1, ROUTE_G2, ROUTE_R1, ROUTE_R2 = range(6)


def _attn_body(*refs, with_router):
    if with_router:
        (x_ref, mix_ref, wout_ref, gx_ref, wq_ref, k_ref, v_ref, wo_ref, gf_ref, rhi_ref, rlo_ref,
         x2_ref, hb_ref, route_ref, cnt_ref, run_cnt) = refs
    else:
        x_ref, mix_ref, wout_ref, gx_ref, wq_ref, k_ref, v_ref, wo_ref, gf_ref, x2_ref, hb_ref = refs
    x1 = x_ref[...] + _dot(mix_ref[...], wout_ref[...])
    h = _rms(x1, gx_ref[...]).astype(BF16)
    q = _dot(h, wq_ref[...])
    outs = []
    for hd in range(XA_HEADS):
        sl = slice(hd * XA_DH, (hd + 1) * XA_DH)
        s = _dot_nt(q[:, sl].astype(BF16), k_ref[:, sl].astype(BF16)) * (XA_DH ** -0.5)
        e = jnp.exp(s - jnp.max(s, axis=-1, keepdims=True))
        p = e / jnp.sum(e, axis=-1, keepdims=True)
        outs.append(_dot(p.astype(BF16), v_ref[:, sl].astype(BF16)))
    o = jnp.concatenate(outs, axis=-1).astype(BF16)
    x2 = x1 + _dot(o, wo_ref[...])
    x2_ref[...] = x2
    h2 = _rms(x2, gf_ref[...])
    h2_hi = h2.astype(BF16)
    if not with_router:
        hb_ref[...] = h2_hi
        return
    hb_ref[...] = h2

    @pl.when((pl.program_id(0) == 0) & (pl.program_id(1) == 0))
    def _():
        run_cnt[...] = jnp.zeros_like(run_cnt)

    h2_lo = (h2 - h2_hi.astype(F32)).astype(BF16)
    logits = _dot(h2_hi, rhi_ref[...]) + _dot(h2_lo, rhi_ref[...]) + _dot(h2_hi, rlo_ref[...])
    tq = logits.shape[0]
    lane = lax.broadcasted_iota(jnp.int32, logits.shape, 1).astype(F32)
    neg = -jnp.inf
    lg = jnp.where(lane < N_EXPERTS, logits, neg)
    m1 = jnp.max(lg, axis=-1, keepdims=True)
    i1 = jnp.min(jnp.where(lg == m1, lane, float(LANE)), axis=-1, keepdims=True)
    lg2 = jnp.where(lane == i1, neg, lg)
    m2 = jnp.max(lg2, axis=-1, keepdims=True)
    i2 = jnp.min(jnp.where(lg2 == m2, lane, float(LANE)), axis=-1, keepdims=True)
    e2 = jnp.exp(m2 - m1)
    g1 = 1.0 / (1.0 + e2)
    g2 = e2 / (1.0 + e2)
    sel = jnp.maximum(jnp.where(lane == i1, 1.0, 0.0), jnp.where(lane == i2, 1.0, 0.0))
    rr = lax.broadcasted_iota(jnp.int32, (tq, tq), 0)
    cc = lax.broadcasted_iota(jnp.int32, (tq, tq), 1)
    before = jnp.where(cc < rr, 1.0, 0.0).astype(BF16)
    rank = _dot(before, sel.astype(BF16)) + run_cnt[...]
    r1 = jnp.sum(jnp.where(lane == i1, rank, 0.0), axis=-1, keepdims=True)
    r2 = jnp.sum(jnp.where(lane == i2, rank, 0.0), axis=-1, keepdims=True)
    run_cnt[...] = run_cnt[...] + jnp.sum(sel, axis=0, keepdims=True)
    cnt_ref[...] = run_cnt[...]
    rec = jnp.zeros_like(logits)
    for ln, val in ((ROUTE_I1, i1), (ROUTE_I2, i2), (ROUTE_G1, g1), (ROUTE_G2, g2), (ROUTE_R1, r1), (ROUTE_R2, r2)):
        rec = jnp.where(lane == float(ln), val, rec)
    route_ref[...] = rec


def _attn(x, mix, mk, mv, layer, lw, g_ffn, router, batch, seq, tq):
    m = batch * seq
    nt = seq // tq
    with_router = router is not None
    row = pl.BlockSpec((tq, D_MODEL), lambda b, t: (b * nt + t, 0))
    vec = pl.BlockSpec((1, D_MODEL), lambda b, t: (0, 0))
    mat = pl.BlockSpec((D_MODEL, D_MODEL), lambda b, t: (0, 0))
    kv = pl.BlockSpec((None, MEM_LEN, D_MODEL), lambda b, t: (layer * batch + b, 0, 0))
    in_specs = [row, row, mat, vec, mat, kv, kv, mat, vec]
    args = [x, mix, lw["w_out"], lw["g_xattn"], lw["wq"], mk, mv, lw["wo"], g_ffn]
    out_specs = [row, row]
    out_shape = [jax.ShapeDtypeStruct((m, D_MODEL), F32),
                 jax.ShapeDtypeStruct((m, D_MODEL), F32 if with_router else BF16)]
    scratch = []
    if with_router:
        rspec = pl.BlockSpec((D_MODEL, LANE), lambda b, t: (0, 0))
        in_specs += [rspec, rspec]
        args += [router[0], router[1]]
        out_specs += [pl.BlockSpec((tq, LANE), lambda b, t: (b * nt + t, 0)),
                      pl.BlockSpec((1, LANE), lambda b, t: (0, 0))]
        out_shape += [jax.ShapeDtypeStruct((m, LANE), F32), jax.ShapeDtypeStruct((1, LANE), F32)]
        scratch = [pltpu.VMEM((1, LANE), F32)]
    sem = ("arbitrary", "arbitrary") if with_router else ("parallel", "parallel")
    return pl.pallas_call(
        functools.partial(_attn_body, with_router=with_router),
        grid=(batch, nt),
        in_specs=in_specs,
        out_specs=out_specs,
        out_shape=out_shape,
        scratch_shapes=scratch,
        compiler_params=_cparams(*sem),
        name="xattn_router" if with_router else "xattn",
    )(*args)


FF_CHUNK = 256


def _swiglu_tile(h, w1_ref, w3_ref, w2_ref):
    y = None
    for f0 in range(0, D_FF, FF_CHUNK):
        a = _dot(h, w1_ref[:, f0:f0 + FF_CHUNK])
        c = _dot(h, w3_ref[:, f0:f0 + FF_CHUNK])
        g = (_silu(a) * c).astype(BF16)
        part = _dot(g, w2_ref[f0:f0 + FF_CHUNK, :])
        y = part if y is None else y + part
    return y


def _ffn_body(hb_ref, x_ref, w1_ref, w3_ref, w2_ref, gfin_ref, o_ref, *, final_norm):
    out = x_ref[...] + _swiglu_tile(hb_ref[...], w1_ref, w3_ref, w2_ref)
    o_ref[...] = _rms(out, gfin_ref[...]) if final_norm else out


def _ffn(hb, x, w1, w3, w2, g_final, tm, final_norm):
    m = x.shape[0]
    row = pl.BlockSpec((tm, D_MODEL), lambda i: (i, 0))
    return pl.pallas_call(
        functools.partial(_ffn_body, final_norm=final_norm),
        grid=(m // tm,),
        in_specs=[
            row, row,
            pl.BlockSpec((D_MODEL, D_FF), lambda i: (0, 0)),
            pl.BlockSpec((D_MODEL, D_FF), lambda i: (0, 0)),
            pl.BlockSpec((D_FF, D_MODEL), lambda i: (0, 0)),
            pl.BlockSpec((1, D_MODEL), lambda i: (0, 0)),
        ],
        out_specs=row,
        out_shape=jax.ShapeDtypeStruct((m, D_MODEL), F32),
        compiler_params=_cparams("parallel"),
        name="ffn",
    )(hb, x, w1, w3, w2, g_final)


def _route_plan(route, cnt, m, tm):
    i1 = route[:, ROUTE_I1].astype(jnp.int32)
    i2 = route[:, ROUTE_I2].astype(jnp.int32)
    r1 = route[:, ROUTE_R1].astype(jnp.int32)
    r2 = route[:, ROUTE_R2].astype(jnp.int32)
    counts = cnt[0, :N_EXPERTS].astype(jnp.int32)
    padded = ((counts + tm - 1) // tm) * tm
    off_end = jnp.cumsum(padded)
    off = off_end - padded
    pos1 = jnp.take(off, i1) + r1
    pos2 = jnp.take(off, i2) + r2
    n_tiles = (2 * m) // tm + N_EXPERTS
    n_used = off_end[-1] // tm
    tile = jnp.minimum(jnp.arange(n_tiles, dtype=jnp.int32), n_used - 1)
    tile_expert = jnp.sum((tile[:, None] * tm >= off_end[None, :]).astype(jnp.int32), axis=1)
    return dict(pos1=pos1, pos2=pos2, tile_expert=tile_expert, n_used=n_used.reshape(1),
                pad_start=off + counts, pad_len=padded - counts, n_tiles=n_tiles, n_slots=n_tiles * tm)


def _dispatch_body(pad_start, pad_len, n_used, pos1_ref, pos2_ref, h_hbm, xs_hbm, zrow, sem, zsem, *, tb, n_tiles):
    i = pl.program_id(0)

    def row_copy(t, p):
        return pltpu.make_async_copy(h_hbm.at[pl.ds(t, 1)], xs_hbm.at[pl.ds(p, 1)], sem)

    def issue(r, c):
        t = i * tb + r
        row_copy(t, pos1_ref[r]).start()
        row_copy(t, pos2_ref[r]).start()
        return c

    lax.fori_loop(0, tb, issue, 0)

    @pl.when(i == 0)
    def _():
        zrow[...] = jnp.zeros_like(zrow)

        def zero_copy(p):
            return pltpu.make_async_copy(zrow.at[pl.ds(0, 1)], xs_hbm.at[pl.ds(p, 1)], zsem)

        def zero_tile(t):
            return pltpu.make_async_copy(zrow, xs_hbm.at[pl.ds(t * tb, tb)], zsem)

        for e in range(N_EXPERTS):
            lax.fori_loop(0, pad_len[e], lambda r, c: (zero_copy(pad_start[e] + r).start(), c)[1], 0)
        lax.fori_loop(n_used[0], n_tiles, lambda t, c: (zero_tile(t).start(), c)[1], 0)
        for e in range(N_EXPERTS):
            lax.fori_loop(0, pad_len[e], lambda r, c: (zero_copy(pad_start[e] + r).wait(), c)[1], 0)
        lax.fori_loop(n_used[0], n_tiles, lambda t, c: (zero_tile(t).wait(), c)[1], 0)

    pltpu.make_async_copy(h_hbm.at[pl.ds(0, 2 * tb)], xs_hbm.at[pl.ds(0, 2 * tb)], sem).wait()


def _dispatch(h2, plan, tb):
    m = h2.shape[0]
    ns = plan["n_slots"]
    idx = pl.BlockSpec((tb,), lambda i, *_: (i,), memory_space=pltpu.SMEM)
    return pl.pallas_call(
        functools.partial(_dispatch_body, tb=tb, n_tiles=plan["n_tiles"]),
        grid_spec=pltpu.PrefetchScalarGridSpec(
            num_scalar_prefetch=3,
            grid=(m // tb,),
            in_specs=[idx, idx, pl.BlockSpec(memory_space=pl.ANY)],
            out_specs=pl.BlockSpec(memory_space=pl.ANY),
            scratch_shapes=[pltpu.VMEM((tb, D_MODEL), F32), pltpu.SemaphoreType.DMA(()),
                            pltpu.SemaphoreType.DMA(())],
        ),
        out_shape=jax.ShapeDtypeStruct((ns, D_MODEL), F32),
        compiler_params=_cparams("arbitrary"),
        name="moe_dispatch",
    )(plan["pad_start"], plan["pad_len"], plan["n_used"], plan["pos1"], plan["pos2"], h2)


def _expert_body(tile_expert, n_used, xs_ref, w1_ref, w3_ref, w2_ref, ys_ref):
    used = pl.program_id(0) < n_used[0]

    @pl.when(used)
    def _():
        ys_ref[...] = _swiglu_tile(xs_ref[...].astype(BF16), w1_ref, w3_ref, w2_ref)

    @pl.when(jnp.logical_not(used))
    def _():
        ys_ref[...] = jnp.zeros_like(ys_ref)


def _expert_ffn(xs, plan, w1, w3, w2, tm):
    ns = xs.shape[0]
    row = pl.BlockSpec((tm, D_MODEL), lambda i, te, nu: (jnp.minimum(i, nu[0] - 1), 0))
    out_row = pl.BlockSpec((tm, D_MODEL), lambda i, te, nu: (i, 0))
    return pl.pallas_call(
        _expert_body,
        grid_spec=pltpu.PrefetchScalarGridSpec(
            num_scalar_prefetch=2,
            grid=(plan["n_tiles"],),
            in_specs=[
                row,
                pl.BlockSpec((None, D_MODEL, D_FF), lambda i, te, nu: (te[i], 0, 0)),
                pl.BlockSpec((None, D_MODEL, D_FF), lambda i, te, nu: (te[i], 0, 0)),
                pl.BlockSpec((None, D_FF, D_MODEL), lambda i, te, nu: (te[i], 0, 0)),
            ],
            out_specs=out_row,
        ),
        out_shape=jax.ShapeDtypeStruct((ns, D_MODEL), F32),
        compiler_params=_cparams("arbitrary"),
        name="moe_experts",
    )(plan["tile_expert"], plan["n_used"], xs, w1, w3, w2)


def _combine_body(pos1_ref, pos2_ref, route_ref, x_ref, gfin_ref, ys_hbm, o_ref, ya, yb, sem, *, tb, final_norm):
    def issue(r, c):
        pltpu.make_async_copy(ys_hbm.at[pl.ds(pos1_ref[r], 1)], ya.at[pl.ds(r, 1)], sem).start()
        pltpu.make_async_copy(ys_hbm.at[pl.ds(pos2_ref[r], 1)], yb.at[pl.ds(r, 1)], sem).start()
        return c

    lax.fori_loop(0, tb, issue, 0)
    pltpu.make_async_copy(ys_hbm.at[pl.ds(0, tb)], ya, sem).wait()
    pltpu.make_async_copy(ys_hbm.at[pl.ds(0, tb)], yb, sem).wait()
    rec = route_ref[...]
    lane = lax.broadcasted_iota(jnp.int32, rec.shape, 1)
    g1 = jnp.sum(jnp.where(lane == ROUTE_G1, rec, 0.0), axis=-1, keepdims=True)
    g2 = jnp.sum(jnp.where(lane == ROUTE_G2, rec, 0.0), axis=-1, keepdims=True)
    out = x_ref[...] + (g1 * ya[...] + g2 * yb[...])
    o_ref[...] = _rms(out, gfin_ref[...]) if final_norm else out


def _combine(x, ys, route, plan, g_final, tb, final_norm):
    m = x.shape[0]
    idx = pl.BlockSpec((tb,), lambda i: (i,), memory_space=pltpu.SMEM)
    row = pl.BlockSpec((tb, D_MODEL), lambda i: (i, 0))
    return pl.pallas_call(
        functools.partial(_combine_body, tb=tb, final_norm=final_norm),
        grid=(m // tb,),
        in_specs=[idx, idx, pl.BlockSpec((tb, LANE), lambda i: (i, 0)), row,
                  pl.BlockSpec((1, D_MODEL), lambda i: (0, 0)), pl.BlockSpec(memory_space=pl.ANY)],
        out_specs=row,
        out_shape=jax.ShapeDtypeStruct((m, D_MODEL), F32),
        scratch_shapes=[pltpu.VMEM((tb, D_MODEL), F32), pltpu.VMEM((tb, D_MODEL), F32),
                        pltpu.SemaphoreType.DMA(())],
        compiler_params=_cparams("arbitrary"),
        name="moe_combine",
    )(plan["pos1"], plan["pos2"], route, x, g_final, ys)


def _moe(h2, x, route, cnt, w1, w3, w2, g_final, tm, final_norm):
    m = x.shape[0]
    plan = _route_plan(route, cnt, m, tm)
    xs = _dispatch(h2, plan, tm)
    ys = _expert_ffn(xs, plan, w1, w3, w2, tm)
    return _combine(x, ys, route, plan, g_final, tm, final_norm)


def _pad_heads(w, dk):
    lead = w.shape[:-1]
    w = w.reshape(lead + (HEADS, dk))
    w = jnp.pad(w, [(0, 0)] * len(lead) + [(0, 0), (0, HEAD_K - dk)])
    return w.reshape(lead + (KL,))


def _block_diag(w):
    nb, n, _ = w.shape
    eye = jnp.eye(nb, dtype=w.dtype)
    return jnp.einsum("ncd,nm->ncmd", w, eye).reshape(nb * n, nb * n)


def _layer_weights(l, P):
    offs = [0]
    for s in IN_SIZES:
        offs.append(offs[-1] + s)
    cols = [P["w_in"][l][:, offs[i]:offs[i + 1]] for i in range(len(IN_SIZES))]
    gq, gk, gv, gr, ga, hq, hf, hi, hg, rx, rg = cols
    w_in = jnp.concatenate([
        _pad_heads(gq, GLA_DK), _pad_heads(gk, GLA_DK), jnp.pad(ga, ((0, 0), (0, GA_PAD - GLA_RANK))), gv, gr,
        hq, hf, hi, hg, rx, rg], axis=1).astype(BF16)
    row = lambda v: v.reshape(1, -1).astype(F32)
    return dict(
        g_mix=row(P["norm_mix"][l]), w_in=w_in,
        wa2=jnp.pad(_pad_heads(P["gla_wa2"][l], GLA_DK), ((0, GA_PAD - GLA_RANK), (0, 0))).astype(BF16),
        ba=row(_pad_heads(P["gla_ba"][l], GLA_DK)), gla_on=row(P["gla_onorm"][l]),
        lb=row(P["lb_all"][l]), hg_on=row(P["hg_onorm"][l]),
        conv_w=jnp.pad(P["rg_conv_w"][l], ((0, SUBLANE - RG_CONV), (0, 0))).astype(F32),
        conv_b=row(P["rg_conv_b"][l]),
        wr=_block_diag(P["rg_wr"][l]).astype(BF16), br=row(P["rg_br"][l]),
        wi=_block_diag(P["rg_wi"][l]).astype(BF16), bi=row(P["rg_bi"][l]),
        lam=row(P["rg_lambda"][l]),
        w_out=P["w_out"][l].astype(BF16), g_xattn=row(P["norm_xattn"][l]),
        wq=P["xa_wq"][l].astype(BF16), wo=P["xa_wo"][l].astype(BF16),
        g_ffn=row(P["norm_ffn"][l]),
    )


def _state_to_kernel(s, dk):
    s = jnp.pad(s, ((0, 0), (0, 0), (0, HEAD_K - dk), (0, 0)))
    eye = jnp.eye(HEADS, dtype=s.dtype)
    b = s.shape[0]
    return jnp.einsum("bhkv,hg->bhvgk", s, eye).reshape(b, VL, KL)


def _state_from_kernel(st, dk):
    b = st.shape[0]
    s = st.reshape(b, HEADS, HEAD_V, HEADS, HEAD_K)
    s = jnp.stack([s[:, h, :, h, :] for h in range(HEADS)], axis=1)
    return jnp.swapaxes(s, -1, -2)[:, :, :dk, :]


def _trunk(x, mem_k, mem_v, s_gla, s_hg, s_rg, s_conv, P, LW):
    batch, seq, _ = x.shape
    depth = len(LW)
    m = batch * seq
    c = min(seq, 128)
    tm = min(m, 512)
    tq = min(seq, 512)
    xf = x.reshape(m, D_MODEL)
    n_gla, n_hg, n_rg, n_cv = [], [], [], []
    for l in range(depth):
        lw = LW[l]
        p = _proj(xf, lw["g_mix"], lw["w_in"], tm)
        mix, sg, sh, hr, cv = _mixer(
            p, lw, _state_to_kernel(s_gla[l], GLA_DK), _state_to_kernel(s_hg[l], HG_DK),
            s_rg[l].reshape(batch, 1, RG_WIDTH), s_conv[l], batch, seq, c)
        moe = l % 2 == 1
        j = l // 2
        router = P["router"][j] if moe else None
        res = _attn(xf, mix, mem_k, mem_v, l, lw, lw["g_ffn"], router, batch, seq, tq)
        last = l == depth - 1
        if moe:
            x2, h2, route, cnt = res
            xf = _moe(h2, x2, route, cnt, P["moe_w1"][j], P["moe_w3"][j], P["moe_w2"][j], P["g_final"], tm, last)
        else:
            x2, hb = res
            xf = _ffn(hb, x2, P["ffd_w1"][j], P["ffd_w3"][j], P["ffd_w2"][j], P["g_final"], tm, last)
        n_gla.append(_state_from_kernel(sg, GLA_DK))
        n_hg.append(_state_from_kernel(sh, HG_DK))
        n_rg.append(hr.reshape(batch, RG_WIDTH))
        n_cv.append(cv)
    return (xf.reshape(batch, seq, D_MODEL), jnp.stack(n_gla), jnp.stack(n_hg), jnp.stack(n_rg), jnp.stack(n_cv))


def kernel(x_prompt, x_sample, state_gla, state_hgrn, state_rglru, state_conv, cache_mem_k, cache_mem_v, mem_prompt, norm_mix, norm_xattn, norm_ffn, norm_mem, norm_final, w_in, gla_wa2, gla_ba, gla_onorm, hg_lb, hg_onorm, rg_conv_w, rg_conv_b, rg_wr, rg_br, rg_wi, rg_bi, rg_lambda, w_out, xa_wq, xa_wk, xa_wv, xa_wo, ffd_w1, ffd_w3, ffd_w2, moe_router, moe_w1, moe_w3, moe_w2):
    depth = w_in.shape[0]
    batch = x_prompt.shape[0]
    pl_ = jax.nn.softmax(hg_lb.astype(F32), axis=0)
    lb_all = jnp.cumsum(pl_, axis=0) - pl_[:1]
    r_hi = jnp.pad(moe_router, ((0, 0), (0, 0), (0, LANE - N_EXPERTS)))
    r_hi_b = r_hi.astype(BF16)
    r_lo_b = (r_hi - r_hi_b.astype(F32)).astype(BF16)
    P = dict(
        w_in=w_in, gla_wa2=gla_wa2, gla_ba=gla_ba, gla_onorm=gla_onorm, lb_all=lb_all, hg_onorm=hg_onorm,
        rg_conv_w=rg_conv_w, rg_conv_b=rg_conv_b, rg_wr=rg_wr, rg_br=rg_br, rg_wi=rg_wi, rg_bi=rg_bi,
        rg_lambda=rg_lambda, w_out=w_out, xa_wq=xa_wq, xa_wo=xa_wo, norm_mix=norm_mix, norm_xattn=norm_xattn,
        norm_ffn=norm_ffn,
        router=[(r_hi_b[j], r_lo_b[j]) for j in range(moe_router.shape[0])],
        ffd_w1=ffd_w1.astype(BF16), ffd_w3=ffd_w3.astype(BF16), ffd_w2=ffd_w2.astype(BF16),
        moe_w1=moe_w1.astype(BF16), moe_w3=moe_w3.astype(BF16), moe_w2=moe_w2.astype(BF16),
        g_final=norm_final.reshape(1, D_MODEL).astype(F32),
    )
    LW = [_layer_weights(l, P) for l in range(depth)]

    mem_flat = mem_prompt.reshape(batch * MEM_LEN, D_MODEL)
    mk, mv = _mem_kv(mem_flat, norm_mem.reshape(depth, 1, D_MODEL).astype(F32), xa_wk.astype(BF16),
                     xa_wv.astype(BF16), 512)
    mem_k_p = mk.reshape(depth, batch, MEM_LEN, XA_HEADS, XA_DH)
    mem_v_p = mv.reshape(depth, batch, MEM_LEN, XA_HEADS, XA_DH)
    dt = x_prompt.dtype
    z_gla = jnp.zeros((depth, batch, GLA_HEADS, GLA_DK, GLA_DV), dt)
    z_hg = jnp.zeros((depth, batch, HG_HEADS, HG_DK, HG_DV), dt)
    z_rg = jnp.zeros((depth, batch, RG_WIDTH), dt)
    z_cv = jnp.zeros((depth, batch, RG_CONV - 1, RG_WIDTH), dt)
    y_prompt, gla_p, hgrn_p, rglru_p, conv_p = _trunk(
        x_prompt, mk.reshape(depth * batch, MEM_LEN, D_MODEL), mv.reshape(depth * batch, MEM_LEN, D_MODEL),
        z_gla, z_hg, z_rg, z_cv, P, LW)
    dec_batch = x_sample.shape[0]
    y_sample, gla_s, hgrn_s, rglru_s, conv_s = _trunk(
        x_sample, cache_mem_k.reshape(depth * dec_batch, MEM_LEN, D_MODEL),
        cache_mem_v.reshape(depth * dec_batch, MEM_LEN, D_MODEL), state_gla, state_hgrn, state_rglru, state_conv,
        P, LW)
    return (y_prompt, y_sample, gla_p, hgrn_p, rglru_p, conv_p, mem_k_p, mem_v_p, gla_s, hgrn_s, rglru_s, conv_s)
```

```python
import functools

import jax
import jax.numpy as jnp
from jax import lax
from jax.experimental import pallas as pl
from jax.experimental.pallas import tpu as pltpu

F32 = jnp.float32
BF16 = jnp.bfloat16

D_MODEL = 1024
GLA_HEADS = 4
GLA_DK = 48
GLA_DV = 96
GLA_RANK = 16
GLA_TAU = 16.0
HG_HEADS = 4
HG_DK = 64
HG_DV = 96
RG_WIDTH = 256
RG_BLOCKS = 4
RG_CONV = 4
RG_C = 8.0
MEM_LEN = 256
XA_HEADS = 4
XA_DH = D_MODEL // XA_HEADS
D_FF = 2816
N_EXPERTS = 8
EPS = 1e-6
IN_SIZES = (192, 192, 384, 384, 16, 256, 256, 384, 384, 256, 256)

LANE = 128
SUBLANE = 8
VMEM_LIMIT_BYTES = 56 * 1024 * 1024

HEADS = 4
HEAD_K = 64
KL = HEADS * HEAD_K
HEAD_V = 96
VL = HEADS * HEAD_V
SUB = 16
GA_PAD = LANE

OFF_GQ = 0
OFF_GK = OFF_GQ + KL
OFF_GA = OFF_GK + KL
OFF_GV = OFF_GA + GA_PAD
OFF_GR = OFF_GV + VL
OFF_HQ = OFF_GR + VL
OFF_HF = OFF_HQ + KL
OFF_HI = OFF_HF + KL
OFF_HG = OFF_HI + VL
OFF_RX = OFF_HG + VL
OFF_RG = OFF_RX + RG_WIDTH
PROJ_W = OFF_RG + RG_WIDTH


def _cparams(*sem):
    return pltpu.CompilerParams(dimension_semantics=sem, vmem_limit_bytes=VMEM_LIMIT_BYTES)


def _dot(a, b):
    return jnp.dot(a, b, preferred_element_type=F32)


def _dot_nt(a, b):
    return lax.dot_general(a, b, (((1,), (1,)), ((), ())), preferred_element_type=F32)


def _dot_tn(a, b):
    return lax.dot_general(a, b, (((0,), (0,)), ((), ())), preferred_element_type=F32)


def _sigmoid(x):
    return 1.0 / (1.0 + jnp.exp(-x))


def _silu(x):
    return x * _sigmoid(x)


def _softplus(x):
    return jnp.maximum(x, 0.0) + jnp.log1p(jnp.exp(-jnp.abs(x)))


def _expm1(x):
    return jnp.tanh(0.5 * x) * (jnp.exp(x) + 1.0)


def _rms(x, g):
    return x * lax.rsqrt(jnp.mean(x * x, axis=-1, keepdims=True) + EPS) * g


def _split3(x):
    hi = x.astype(BF16)
    r1 = x - hi.astype(F32)
    mid = r1.astype(BF16)
    lo = (r1 - mid.astype(F32)).astype(BF16)
    return hi, mid, lo


def _proj_body(x_ref, g_ref, w_ref, o_ref):
    h = _rms(x_ref[...], g_ref[...]).astype(BF16)
    o_ref[...] = _dot(h, w_ref[...])


def _proj(x, g, w, tm):
    m, d = x.shape
    n = w.shape[1]
    return pl.pallas_call(
        _proj_body,
        grid=(m // tm,),
        in_specs=[
            pl.BlockSpec((tm, d), lambda i: (i, 0)),
            pl.BlockSpec((1, d), lambda i: (0, 0)),
            pl.BlockSpec((d, n), lambda i: (0, 0)),
        ],
        out_specs=pl.BlockSpec((tm, n), lambda i: (i, 0)),
        out_shape=jax.ShapeDtypeStruct((m, n), F32),
        compiler_params=_cparams("parallel"),
        name="in_proj",
    )(x, g, w)


def _memkv_body(m_ref, g_ref, wk_ref, wv_ref, k_ref, v_ref):
    h = _rms(m_ref[...], g_ref[...]).astype(BF16)
    k_ref[...] = _dot(h, wk_ref[...])
    v_ref[...] = _dot(h, wv_ref[...])


def _mem_kv(mem, g, wk, wv, tm):
    m, d = mem.shape
    nl = g.shape[0]
    out = jax.ShapeDtypeStruct((nl, m, d), F32)
    return pl.pallas_call(
        _memkv_body,
        grid=(nl, m // tm),
        in_specs=[
            pl.BlockSpec((tm, d), lambda l, i: (i, 0)),
            pl.BlockSpec((None, 1, d), lambda l, i: (l, 0, 0)),
            pl.BlockSpec((None, d, d), lambda l, i: (l, 0, 0)),
            pl.BlockSpec((None, d, d), lambda l, i: (l, 0, 0)),
        ],
        out_specs=[
            pl.BlockSpec((None, tm, d), lambda l, i: (l, i, 0)),
            pl.BlockSpec((None, tm, d), lambda l, i: (l, i, 0)),
        ],
        out_shape=[out, out],
        compiler_params=_cparams("parallel", "parallel"),
        name="mem_kv",
    )(mem, g, wk, wv)


def _key_head(shape, dim):
    return lax.shift_right_logical(lax.broadcasted_iota(jnp.int32, shape, dim), 6)


def _val_head(shape, dim):
    i = lax.broadcasted_iota(jnp.int32, shape, dim)
    one = jnp.ones(shape, jnp.int32)
    zero = jnp.zeros(shape, jnp.int32)
    return (jnp.where(i >= HEAD_V, one, zero) + jnp.where(i >= 2 * HEAD_V, one, zero)
            + jnp.where(i >= 3 * HEAD_V, one, zero))


def _gated_chunk(q, k, v, lg, st_ref, c):
    nsub = c // SUB
    row = lax.broadcasted_iota(jnp.int32, (c, c), 0)
    col = lax.broadcasted_iota(jnp.int32, (c, c), 1)
    ltri = jnp.where(col <= row, 1.0, 0.0).astype(BF16)
    hi, mid, lo = _split3(lg)
    b = _dot(ltri, hi) + _dot(ltri, mid) + _dot(ltri, lo)
    b_last = b[c - 1:c, :]

    s_t = st_ref[...]
    o_inter = _dot_nt((q * jnp.exp(b)).astype(BF16), s_t.astype(BF16))
    k_end = (k * jnp.exp(b_last - b)).astype(BF16)
    upd = _dot_tn(v.astype(BF16), k_end)
    same_head = _val_head((VL, KL), 0) == _key_head((VL, KL), 1)
    st_ref[...] = s_t * jnp.exp(b_last) + jnp.where(same_head, upd, 0.0)

    khead = _key_head((SUB, KL), 1)
    vhead = _val_head((SUB, VL), 1)
    rr = lax.broadcasted_iota(jnp.int32, (SUB, HEADS * SUB), 0)
    cc = lax.broadcasted_iota(jnp.int32, (SUB, HEADS * SUB), 1)
    causal = jnp.bitwise_and(cc, SUB - 1) <= rr
    pieces = [o_inter[i * SUB:(i + 1) * SUB, :] for i in range(nsub)]
    for j in range(nsub):
        r0 = j * SUB
        e_j = b[r0 + SUB - 1:r0 + SUB, :]
        e_prev = b[r0 - 1:r0, :] if j else jnp.zeros_like(e_j)
        m = 0.5 * (e_prev + e_j)
        lhs = (q[r0:, :] * jnp.exp(b[r0:, :] - m)).astype(BF16)
        ku = k[r0:r0 + SUB, :] * jnp.exp(m - b[r0:r0 + SUB, :])
        ks = jnp.concatenate([jnp.where(khead == h, ku, 0.0) for h in range(HEADS)], axis=0).astype(BF16)
        p = _dot_nt(lhs, ks)
        vj = v[r0:r0 + SUB, :]
        vs = jnp.concatenate([jnp.where(vhead == h, vj, 0.0) for h in range(HEADS)], axis=0).astype(BF16)
        p_diag = jnp.where(causal, p[:SUB, :], 0.0)
        if c - r0 > SUB:
            p = jnp.concatenate([p_diag, p[SUB:, :]], axis=0)
        else:
            p = p_diag
        pv = _dot(p.astype(BF16), vs)
        for i in range(j, nsub):
            pieces[i] = pieces[i] + pv[(i - j) * SUB:(i - j + 1) * SUB, :]
    return jnp.concatenate(pieces, axis=0) if nsub > 1 else pieces[0]


def _head_rms(o, gain):
    seg = (_val_head((VL, VL), 0) == _val_head((VL, VL), 1))
    ones = jnp.where(seg, 1.0, 0.0).astype(BF16)
    sq = o * o
    hi = sq.astype(BF16)
    lo = (sq - hi.astype(F32)).astype(BF16)
    ms = (_dot(hi, ones) + _dot(lo, ones)) * (1.0 / HEAD_V)
    return o * lax.rsqrt(ms + EPS) * gain


def _mixer_body(p_ref, wa2_ref, ba_ref, gon_ref, lb_ref, hon_ref, cw_ref, cb_ref, wr_ref, br_ref, wi_ref,
                bi_ref, lam_ref, sg0_ref, sh0_ref, hr0_ref, cv0_ref,
                mix_ref, sg_ref, sh_ref, hr_ref, cv_ref,
                sg_t, sh_t, h_car, xc, *, c):
    t = pl.program_id(1)
    nt = pl.num_programs(1)
    npad = SUBLANE
    ncar = RG_CONV - 1

    @pl.when(t == 0)
    def _():
        sg_t[...] = sg0_ref[...]
        sh_t[...] = sh0_ref[...]
        h_car[...] = hr0_ref[...]
        xc[npad - ncar:npad, :] = cv0_ref[...]

    gq = p_ref[:, OFF_GQ:OFF_GQ + KL] * (GLA_DK ** -0.5)
    gk = p_ref[:, OFF_GK:OFF_GK + KL]
    ga = p_ref[:, OFF_GA:OFF_GA + GA_PAD]
    z = _dot(ga.astype(BF16), wa2_ref[...]) + ba_ref[...]
    lg = -_softplus(-z) * (1.0 / GLA_TAU)
    o = _gated_chunk(gq, gk, p_ref[:, OFF_GV:OFF_GV + VL], lg, sg_t, c)
    o = _head_rms(o, gon_ref[...]) * _silu(p_ref[:, OFF_GR:OFF_GR + VL])
    mix_ref[:, 0:VL] = o.astype(BF16)

    z = p_ref[:, OFF_HF:OFF_HF + KL]
    lb = lb_ref[...]
    f = lb + (1.0 - lb) * _sigmoid(z)
    hk = (1.0 - lb) * _sigmoid(-z)
    hq = _silu(p_ref[:, OFF_HQ:OFF_HQ + KL])
    o = _gated_chunk(hq, hk, p_ref[:, OFF_HI:OFF_HI + VL], jnp.log(f), sh_t, c)
    o = _head_rms(o, hon_ref[...]) * _silu(p_ref[:, OFF_HG:OFF_HG + VL])
    mix_ref[:, VL:2 * VL] = o.astype(BF16)

    xb = p_ref[:, OFF_RX:OFF_RX + RG_WIDTH]
    xc[npad:npad + c, :] = xb
    u = cb_ref[...]
    for j in range(RG_CONV):
        u = u + xc[npad - ncar + j:npad - ncar + j + c, :] * cw_ref[j:j + 1, :]
    xc[npad - ncar:npad, :] = xc[npad + c - ncar:npad + c, :]
    ub = u.astype(BF16)
    r = _sigmoid(_dot(ub, wr_ref[...]) + br_ref[...])
    ig = _sigmoid(_dot(ub, wi_ref[...]) + bi_ref[...])
    log_a = (-RG_C * r) * _softplus(-lam_ref[...])
    a = jnp.exp(log_a)
    beta = jnp.sqrt(jnp.maximum(-_expm1(2.0 * log_a), 1e-12))
    bt = beta * (ig * u)
    rows = lax.broadcasted_iota(jnp.int32, (c, RG_WIDTH), 0)
    d = 1
    while d < c:
        keep = rows >= d
        a_sh = jnp.where(keep, pltpu.roll(a, d, 0), 1.0)
        b_sh = jnp.where(keep, pltpu.roll(bt, d, 0), 0.0)
        bt = a * b_sh + bt
        a = a * a_sh
        d *= 2
    h = bt + a * h_car[...]
    h_car[...] = h[c - 1:c, :]
    gb = p_ref[:, OFF_RG:OFF_RG + RG_WIDTH]
    gelu = gb * (0.5 * (1.0 + jnp.tanh(0.7978845608028654 * (gb + 0.044715 * (gb * gb * gb)))))
    mix_ref[:, 2 * VL:2 * VL + RG_WIDTH] = (h * gelu).astype(BF16)

    @pl.when(t == nt - 1)
    def _():
        sg_ref[...] = sg_t[...]
        sh_ref[...] = sh_t[...]
        hr_ref[...] = h_car[...]
        cv_ref[...] = xc[npad - ncar:npad, :]


def _mixer(p, lw, sg0, sh0, hr0, cv0, batch, seq, c):
    nt = seq // c
    vec = lambda n: pl.BlockSpec((1, n), lambda b, t: (0, 0))
    mat = lambda r, n: pl.BlockSpec((r, n), lambda b, t: (0, 0))
    st = lambda r, n: pl.BlockSpec((None, r, n), lambda b, t: (b, 0, 0))
    in_specs = [
        pl.BlockSpec((c, PROJ_W), lambda b, t: (b * nt + t, 0)),
        mat(GA_PAD, KL), vec(KL), vec(VL), vec(KL), vec(VL), mat(SUBLANE, RG_WIDTH), vec(RG_WIDTH),
        mat(RG_WIDTH, RG_WIDTH), vec(RG_WIDTH), mat(RG_WIDTH, RG_WIDTH), vec(RG_WIDTH), vec(RG_WIDTH),
        st(VL, KL), st(VL, KL), st(1, RG_WIDTH), st(RG_CONV - 1, RG_WIDTH),
    ]
    out_specs = [
        pl.BlockSpec((c, D_MODEL), lambda b, t: (b * nt + t, 0)),
        st(VL, KL), st(VL, KL), st(1, RG_WIDTH), st(RG_CONV - 1, RG_WIDTH),
    ]
    out_shape = [
        jax.ShapeDtypeStruct((batch * seq, D_MODEL), BF16),
        jax.ShapeDtypeStruct((batch, VL, KL), F32),
        jax.ShapeDtypeStruct((batch, VL, KL), F32),
        jax.ShapeDtypeStruct((batch, 1, RG_WIDTH), F32),
        jax.ShapeDtypeStruct((batch, RG_CONV - 1, RG_WIDTH), F32),
    ]
    scratch = [
        pltpu.VMEM((VL, KL), F32), pltpu.VMEM((VL, KL), F32), pltpu.VMEM((1, RG_WIDTH), F32),
        pltpu.VMEM((c + SUBLANE, RG_WIDTH), F32),
    ]
    return pl.pallas_call(
        functools.partial(_mixer_body, c=c),
        grid=(batch, nt),
        in_specs=in_specs,
        out_specs=out_specs,
        out_shape=out_shape,
        scratch_shapes=scratch,
        compiler_params=_cparams("parallel", "arbitrary"),
        name="mixer",
    )(p, lw["wa2"], lw["ba"], lw["gla_on"], lw["lb"], lw["hg_on"], lw["conv_w"], lw["conv_b"], lw["wr"],
      lw["br"], lw["wi"], lw["bi"], lw["lam"], sg0, sh0, hr0, cv0)


ROUTE_I1, ROUTE_I2, ROUTE_G1, ROUTE_G2, ROUTE_R1, ROUTE_R2 = range(6)


def _attn_body(*refs, with_router):
    if with_router:
        (x_ref, mix_ref, wout_ref, gx_ref, wq_ref, k_ref, v_ref, wo_ref, gf_ref, rhi_ref, rlo_ref,
         x2_ref, hb_ref, route_ref, cnt_ref, run_cnt) = refs
    else:
        x_ref, mix_ref, wout_ref, gx_ref, wq_ref, k_ref, v_ref, wo_ref, gf_ref, x2_ref, hb_ref = refs
    x1 = x_ref[...] + _dot(mix_ref[...], wout_ref[...])
    h = _rms(x1, gx_ref[...]).astype(BF16)
    q = _dot(h, wq_ref[...])
    outs = []
    for hd in range(XA_HEADS):
        sl = slice(hd * XA_DH, (hd + 1) * XA_DH)
        s = _dot_nt(q[:, sl].astype(BF16), k_ref[:, sl].astype(BF16)) * (XA_DH ** -0.5)
        e = jnp.exp(s - jnp.max(s, axis=-1, keepdims=True))
        p = e / jnp.sum(e, axis=-1, keepdims=True)
        outs.append(_dot(p.astype(BF16), v_ref[:, sl].astype(BF16)))
    o = jnp.concatenate(outs, axis=-1).astype(BF16)
    x2 = x1 + _dot(o, wo_ref[...])
    x2_ref[...] = x2
    h2 = _rms(x2, gf_ref[...])
    h2_hi = h2.astype(BF16)
    if not with_router:
        hb_ref[...] = h2_hi
        return
    hb_ref[...] = h2

    @pl.when((pl.program_id(0) == 0) & (pl.program_id(1) == 0))
    def _():
        run_cnt[...] = jnp.zeros_like(run_cnt)

    h2_lo = (h2 - h2_hi.astype(F32)).astype(BF16)
    logits = _dot(h2_hi, rhi_ref[...]) + _dot(h2_lo, rhi_ref[...]) + _dot(h2_hi, rlo_ref[...])
    tq = logits.shape[0]
    lane = lax.broadcasted_iota(jnp.int32, logits.shape, 1).astype(F32)
    neg = -jnp.inf
    lg = jnp.where(lane < N_EXPERTS, logits, neg)
    m1 = jnp.max(lg, axis=-1, keepdims=True)
    i1 = jnp.min(jnp.where(lg == m1, lane, float(LANE)), axis=-1, keepdims=True)
    lg2 = jnp.where(lane == i1, neg, lg)
    m2 = jnp.max(lg2, axis=-1, keepdims=True)
    i2 = jnp.min(jnp.where(lg2 == m2, lane, float(LANE)), axis=-1, keepdims=True)
    e2 = jnp.exp(m2 - m1)
    g1 = 1.0 / (1.0 + e2)
    g2 = e2 / (1.0 + e2)
    sel = jnp.maximum(jnp.where(lane == i1, 1.0, 0.0), jnp.where(lane == i2, 1.0, 0.0))
    rr = lax.broadcasted_iota(jnp.int32, (tq, tq), 0)
    cc = lax.broadcasted_iota(jnp.int32, (tq, tq), 1)
    before = jnp.where(cc < rr, 1.0, 0.0).astype(BF16)
    rank = _dot(before, sel.astype(BF16)) + run_cnt[...]
    r1 = jnp.sum(jnp.where(lane == i1, rank, 0.0), axis=-1, keepdims=True)
    r2 = jnp.sum(jnp.where(lane == i2, rank, 0.0), axis=-1, keepdims=True)
    run_cnt[...] = run_cnt[...] + jnp.sum(sel, axis=0, keepdims=True)
    cnt_ref[...] = run_cnt[...]
    rec = jnp.zeros_like(logits)
    for ln, val in ((ROUTE_I1, i1), (ROUTE_I2, i2), (ROUTE_G1, g1), (ROUTE_G2, g2), (ROUTE_R1, r1), (ROUTE_R2, r2)):
        rec = jnp.where(lane == float(ln), val, rec)
    route_ref[...] = rec


def _attn(x, mix, mk, mv, layer, lw, g_ffn, router, batch, seq, tq):
    m = batch * seq
    nt = seq // tq
    with_router = router is not None
    row = pl.BlockSpec((tq, D_MODEL), lambda b, t: (b * nt + t, 0))
    vec = pl.BlockSpec((1, D_MODEL), lambda b, t: (0, 0))
    mat = pl.BlockSpec((D_MODEL, D_MODEL), lambda b, t: (0, 0))
    kv = pl.BlockSpec((None, MEM_LEN, D_MODEL), lambda b, t: (layer * batch + b, 0, 0))
    in_specs = [row, row, mat, vec, mat, kv, kv, mat, vec]
    args = [x, mix, lw["w_out"], lw["g_xattn"], lw["wq"], mk, mv, lw["wo"], g_ffn]
    out_specs = [row, row]
    out_shape = [jax.ShapeDtypeStruct((m, D_MODEL), F32),
                 jax.ShapeDtypeStruct((m, D_MODEL), F32 if with_router else BF16)]
    scratch = []
    if with_router:
        rspec = pl.BlockSpec((D_MODEL, LANE), lambda b, t: (0, 0))
        in_specs += [rspec, rspec]
        args += [router[0], router[1]]
        out_specs += [pl.BlockSpec((tq, LANE), lambda b, t: (b * nt + t, 0)),
                      pl.BlockSpec((1, LANE), lambda b, t: (0, 0))]
        out_shape += [jax.ShapeDtypeStruct((m, LANE), F32), jax.ShapeDtypeStruct((1, LANE), F32)]
        scratch = [pltpu.VMEM((1, LANE), F32)]
    sem = ("arbitrary", "arbitrary") if with_router else ("parallel", "parallel")
    return pl.pallas_call(
        functools.partial(_attn_body, with_router=with_router),
        grid=(batch, nt),
        in_specs=in_specs,
        out_specs=out_specs,
        out_shape=out_shape,
        scratch_shapes=scratch,
        compiler_params=_cparams(*sem),
        name="xattn_router" if with_router else "xattn",
    )(*args)


FF_CHUNK = 256


def _swiglu_tile(h, w1_ref, w3_ref, w2_ref):
    y = None
    for f0 in range(0, D_FF, FF_CHUNK):
        a = _dot(h, w1_ref[:, f0:f0 + FF_CHUNK])
        c = _dot(h, w3_ref[:, f0:f0 + FF_CHUNK])
        g = (_silu(a) * c).astype(BF16)
        part = _dot(g, w2_ref[f0:f0 + FF_CHUNK, :])
        y = part if y is None else y + part
    return y


def _ffn_body(hb_ref, x_ref, w1_ref, w3_ref, w2_ref, gfin_ref, o_ref, *, final_norm):
    out = x_ref[...] + _swiglu_tile(hb_ref[...], w1_ref, w3_ref, w2_ref)
    o_ref[...] = _rms(out, gfin_ref[...]) if final_norm else out


def _ffn(hb, x, w1, w3, w2, g_final, tm, final_norm):
    m = x.shape[0]
    row = pl.BlockSpec((tm, D_MODEL), lambda i: (i, 0))
    return pl.pallas_call(
        functools.partial(_ffn_body, final_norm=final_norm),
        grid=(m // tm,),
        in_specs=[
            row, row,
            pl.BlockSpec((D_MODEL, D_FF), lambda i: (0, 0)),
            pl.BlockSpec((D_MODEL, D_FF), lambda i: (0, 0)),
            pl.BlockSpec((D_FF, D_MODEL), lambda i: (0, 0)),
            pl.BlockSpec((1, D_MODEL), lambda i: (0, 0)),
        ],
        out_specs=row,
        out_shape=jax.ShapeDtypeStruct((m, D_MODEL), F32),
        compiler_params=_cparams("parallel"),
        name="ffn",
    )(hb, x, w1, w3, w2, g_final)


def _route_plan(route, cnt, m, tm):
    i1 = route[:, ROUTE_I1].astype(jnp.int32)
    i2 = route[:, ROUTE_I2].astype(jnp.int32)
    r1 = route[:, ROUTE_R1].astype(jnp.int32)
    r2 = route[:, ROUTE_R2].astype(jnp.int32)
    counts = cnt[0, :N_EXPERTS].astype(jnp.int32)
    padded = ((counts + tm - 1) // tm) * tm
    off_end = jnp.cumsum(padded)
    off = off_end - padded
    pos1 = jnp.take(off, i1) + r1
    pos2 = jnp.take(off, i2) + r2
    n_tiles = (2 * m) // tm + N_EXPERTS
    n_used = off_end[-1] // tm
    tile = jnp.minimum(jnp.arange(n_tiles, dtype=jnp.int32), n_used - 1)
    tile_expert = jnp.sum((tile[:, None] * tm >= off_end[None, :]).astype(jnp.int32), axis=1)
    return dict(pos1=pos1, pos2=pos2, tile_expert=tile_expert, n_used=n_used.reshape(1),
                pad_start=off + counts, pad_len=padded - counts, n_tiles=n_tiles, n_slots=n_tiles * tm)


def _dispatch_body(pad_start, pad_len, n_used, pos1_ref, pos2_ref, h_ref, xs_hbm, zrow, sem, zsem, *, tb, n_tiles):
    i = pl.program_id(0)

    def row_copy(r, p):
        return pltpu.make_async_copy(h_ref.at[pl.ds(r, 1)], xs_hbm.at[pl.ds(p, 1)], sem)

    def issue(r, c):
        row_copy(r, pos1_ref[r]).start()
        row_copy(r, pos2_ref[r]).start()
        return c

    lax.fori_loop(0, tb, issue, 0)

    @pl.when(i == 0)
    def _():
        zrow[...] = jnp.zeros_like(zrow)

        def zero_copy(p):
            return pltpu.make_async_copy(zrow.at[pl.ds(0, 1)], xs_hbm.at[pl.ds(p, 1)], zsem)

        def zero_tile(t):
            return pltpu.make_async_copy(zrow, xs_hbm.at[pl.ds(t * tb, tb)], zsem)

        for e in range(N_EXPERTS):
            lax.fori_loop(0, pad_len[e], lambda r, c: (zero_copy(pad_start[e] + r).start(), c)[1], 0)
        lax.fori_loop(n_used[0], n_tiles, lambda t, c: (zero_tile(t).start(), c)[1], 0)
        for e in range(N_EXPERTS):
            lax.fori_loop(0, pad_len[e], lambda r, c: (zero_copy(pad_start[e] + r).wait(), c)[1], 0)
        lax.fori_loop(n_used[0], n_tiles, lambda t, c: (zero_tile(t).wait(), c)[1], 0)

    pltpu.make_async_copy(h_ref, xs_hbm.at[pl.ds(0, tb)], sem).wait()
    pltpu.make_async_copy(h_ref, xs_hbm.at[pl.ds(0, tb)], sem).wait()


def _dispatch(h2, plan, tb):
    m = h2.shape[0]
    ns = plan["n_slots"]
    idx = pl.BlockSpec((tb,), lambda i, *_: (i,), memory_space=pltpu.SMEM)
    return pl.pallas_call(
        functools.partial(_dispatch_body, tb=tb, n_tiles=plan["n_tiles"]),
        grid_spec=pltpu.PrefetchScalarGridSpec(
            num_scalar_prefetch=3,
            grid=(m // tb,),
            in_specs=[idx, idx, pl.BlockSpec((tb, D_MODEL), lambda i, *_: (i, 0))],
            out_specs=pl.BlockSpec(memory_space=pl.ANY),
            scratch_shapes=[pltpu.VMEM((tb, D_MODEL), F32), pltpu.SemaphoreType.DMA(()),
                            pltpu.SemaphoreType.DMA(())],
        ),
        out_shape=jax.ShapeDtypeStruct((ns, D_MODEL), F32),
        compiler_params=_cparams("arbitrary"),
        name="moe_dispatch",
    )(plan["pad_start"], plan["pad_len"], plan["n_used"], plan["pos1"], plan["pos2"], h2)


def _expert_body(tile_expert, n_used, xs_ref, w1_ref, w3_ref, w2_ref, ys_ref):
    used = pl.program_id(0) < n_used[0]

    @pl.when(used)
    def _():
        ys_ref[...] = _swiglu_tile(xs_ref[...].astype(BF16), w1_ref, w3_ref, w2_ref)

    @pl.when(jnp.logical_not(used))
    def _():
        ys_ref[...] = jnp.zeros_like(ys_ref)


def _expert_ffn(xs, plan, w1, w3, w2, tm):
    ns = xs.shape[0]
    row = pl.BlockSpec((tm, D_MODEL), lambda i, te, nu: (jnp.minimum(i, nu[0] - 1), 0))
    out_row = pl.BlockSpec((tm, D_MODEL), lambda i, te, nu: (i, 0))
    return pl.pallas_call(
        _expert_body,
        grid_spec=pltpu.PrefetchScalarGridSpec(
            num_scalar_prefetch=2,
            grid=(plan["n_tiles"],),
            in_specs=[
                row,
                pl.BlockSpec((None, D_MODEL, D_FF), lambda i, te, nu: (te[i], 0, 0)),
                pl.BlockSpec((None, D_MODEL, D_FF), lambda i, te, nu: (te[i], 0, 0)),
                pl.BlockSpec((None, D_FF, D_MODEL), lambda i, te, nu: (te[i], 0, 0)),
            ],
            out_specs=out_row,
        ),
        out_shape=jax.ShapeDtypeStruct((ns, D_MODEL), F32),
        compiler_params=_cparams("arbitrary"),
        name="moe_experts",
    )(plan["tile_expert"], plan["n_used"], xs, w1, w3, w2)


def _combine_body(pos1_ref, pos2_ref, route_ref, x_ref, gfin_ref, ys_hbm, o_ref, ya, yb, sem, *, tb, final_norm):
    def issue(r, c):
        pltpu.make_async_copy(ys_hbm.at[pl.ds(pos1_ref[r], 1)], ya.at[pl.ds(r, 1)], sem).start()
        pltpu.make_async_copy(ys_hbm.at[pl.ds(pos2_ref[r], 1)], yb.at[pl.ds(r, 1)], sem).start()
        return c

    lax.fori_loop(0, tb, issue, 0)
    pltpu.make_async_copy(ys_hbm.at[pl.ds(0, tb)], ya, sem).wait()
    pltpu.make_async_copy(ys_hbm.at[pl.ds(0, tb)], yb, sem).wait()
    rec = route_ref[...]
    lane = lax.broadcasted_iota(jnp.int32, rec.shape, 1)
    g1 = jnp.sum(jnp.where(lane == ROUTE_G1, rec, 0.0), axis=-1, keepdims=True)
    g2 = jnp.sum(jnp.where(lane == ROUTE_G2, rec, 0.0), axis=-1, keepdims=True)
    out = x_ref[...] + (g1 * ya[...] + g2 * yb[...])
    o_ref[...] = _rms(out, gfin_ref[...]) if final_norm else out


def _combine(x, ys, route, plan, g_final, tb, final_norm):
    m = x.shape[0]
    idx = pl.BlockSpec((tb,), lambda i: (i,), memory_space=pltpu.SMEM)
    row = pl.BlockSpec((tb, D_MODEL), lambda i: (i, 0))
    return pl.pallas_call(
        functools.partial(_combine_body, tb=tb, final_norm=final_norm),
        grid=(m // tb,),
        in_specs=[idx, idx, pl.BlockSpec((tb, LANE), lambda i: (i, 0)), row,
                  pl.BlockSpec((1, D_MODEL), lambda i: (0, 0)), pl.BlockSpec(memory_space=pl.ANY)],
        out_specs=row,
        out_shape=jax.ShapeDtypeStruct((m, D_MODEL), F32),
        scratch_shapes=[pltpu.VMEM((tb, D_MODEL), F32), pltpu.VMEM((tb, D_MODEL), F32),
                        pltpu.SemaphoreType.DMA(())],
        compiler_params=_cparams("arbitrary"),
        name="moe_combine",
    )(plan["pos1"], plan["pos2"], route, x, g_final, ys)


def _moe(h2, x, route, cnt, w1, w3, w2, g_final, tm, final_norm):
    m = x.shape[0]
    plan = _route_plan(route, cnt, m, tm)
    xs = _dispatch(h2, plan, tm)
    ys = _expert_ffn(xs, plan, w1, w3, w2, tm)
    return _combine(x, ys, route, plan, g_final, tm, final_norm)


def _pad_heads(w, dk):
    lead = w.shape[:-1]
    w = w.reshape(lead + (HEADS, dk))
    w = jnp.pad(w, [(0, 0)] * len(lead) + [(0, 0), (0, HEAD_K - dk)])
    return w.reshape(lead + (KL,))


def _block_diag(w):
    nb, n, _ = w.shape
    eye = jnp.eye(nb, dtype=w.dtype)
    return jnp.einsum("ncd,nm->ncmd", w, eye).reshape(nb * n, nb * n)


def _layer_weights(l, P):
    offs = [0]
    for s in IN_SIZES:
        offs.append(offs[-1] + s)
    cols = [P["w_in"][l][:, offs[i]:offs[i + 1]] for i in range(len(IN_SIZES))]
    gq, gk, gv, gr, ga, hq, hf, hi, hg, rx, rg = cols
    w_in = jnp.concatenate([
        _pad_heads(gq, GLA_DK), _pad_heads(gk, GLA_DK), jnp.pad(ga, ((0, 0), (0, GA_PAD - GLA_RANK))), gv, gr,
        hq, hf, hi, hg, rx, rg], axis=1).astype(BF16)
    row = lambda v: v.reshape(1, -1).astype(F32)
    return dict(
        g_mix=row(P["norm_mix"][l]), w_in=w_in,
        wa2=jnp.pad(_pad_heads(P["gla_wa2"][l], GLA_DK), ((0, GA_PAD - GLA_RANK), (0, 0))).astype(BF16),
        ba=row(_pad_heads(P["gla_ba"][l], GLA_DK)), gla_on=row(P["gla_onorm"][l]),
        lb=row(P["lb_all"][l]), hg_on=row(P["hg_onorm"][l]),
        conv_w=jnp.pad(P["rg_conv_w"][l], ((0, SUBLANE - RG_CONV), (0, 0))).astype(F32),
        conv_b=row(P["rg_conv_b"][l]),
        wr=_block_diag(P["rg_wr"][l]).astype(BF16), br=row(P["rg_br"][l]),
        wi=_block_diag(P["rg_wi"][l]).astype(BF16), bi=row(P["rg_bi"][l]),
        lam=row(P["rg_lambda"][l]),
        w_out=P["w_out"][l].astype(BF16), g_xattn=row(P["norm_xattn"][l]),
        wq=P["xa_wq"][l].astype(BF16), wo=P["xa_wo"][l].astype(BF16),
        g_ffn=row(P["norm_ffn"][l]),
    )


def _state_to_kernel(s, dk):
    s = jnp.pad(s, ((0, 0), (0, 0), (0, HEAD_K - dk), (0, 0)))
    eye = jnp.eye(HEADS, dtype=s.dtype)
    b = s.shape[0]
    return jnp.einsum("bhkv,hg->bhvgk", s, eye).reshape(b, VL, KL)


def _state_from_kernel(st, dk):
    b = st.shape[0]
    s = st.reshape(b, HEADS, HEAD_V, HEADS, HEAD_K)
    s = jnp.stack([s[:, h, :, h, :] for h in range(HEADS)], axis=1)
    return jnp.swapaxes(s, -1, -2)[:, :, :dk, :]


def _trunk(x, mem_k, mem_v, s_gla, s_hg, s_rg, s_conv, P, LW):
    batch, seq, _ = x.shape
    depth = len(LW)
    m = batch * seq
    c = min(seq, 128)
    tm = min(m, 512)
    tq = min(seq, 512)
    xf = x.reshape(m, D_MODEL)
    n_gla, n_hg, n_rg, n_cv = [], [], [], []
    for l in range(depth):
        lw = LW[l]
        p = _proj(xf, lw["g_mix"], lw["w_in"], tm)
        mix, sg, sh, hr, cv = _mixer(
            p, lw, _state_to_kernel(s_gla[l], GLA_DK), _state_to_kernel(s_hg[l], HG_DK),
            s_rg[l].reshape(batch, 1, RG_WIDTH), s_conv[l], batch, seq, c)
        moe = l % 2 == 1
        j = l // 2
        router = P["router"][j] if moe else None
        res = _attn(xf, mix, mem_k, mem_v, l, lw, lw["g_ffn"], router, batch, seq, tq)
        last = l == depth - 1
        if moe:
            x2, h2, route, cnt = res
            xf = _moe(h2, x2, route, cnt, P["moe_w1"][j], P["moe_w3"][j], P["moe_w2"][j], P["g_final"], tm, last)
        else:
            x2, hb = res
            xf = _ffn(hb, x2, P["ffd_w1"][j], P["ffd_w3"][j], P["ffd_w2"][j], P["g_final"], tm, last)
        n_gla.append(_state_from_kernel(sg, GLA_DK))
        n_hg.append(_state_from_kernel(sh, HG_DK))
        n_rg.append(hr.reshape(batch, RG_WIDTH))
        n_cv.append(cv)
    return (xf.reshape(batch, seq, D_MODEL), jnp.stack(n_gla), jnp.stack(n_hg), jnp.stack(n_rg), jnp.stack(n_cv))


def kernel(x_prompt, x_sample, state_gla, state_hgrn, state_rglru, state_conv, cache_mem_k, cache_mem_v, mem_prompt, norm_mix, norm_xattn, norm_ffn, norm_mem, norm_final, w_in, gla_wa2, gla_ba, gla_onorm, hg_lb, hg_onorm, rg_conv_w, rg_conv_b, rg_wr, rg_br, rg_wi, rg_bi, rg_lambda, w_out, xa_wq, xa_wk, xa_wv, xa_wo, ffd_w1, ffd_w3, ffd_w2, moe_router, moe_w1, moe_w3, moe_w2):
    depth = w_in.shape[0]
    batch = x_prompt.shape[0]
    pl_ = jax.nn.softmax(hg_lb.astype(F32), axis=0)
    lb_all = jnp.cumsum(pl_, axis=0) - pl_[:1]
    r_hi = jnp.pad(moe_router, ((0, 0), (0, 0), (0, LANE - N_EXPERTS)))
    r_hi_b = r_hi.astype(BF16)
    r_lo_b = (r_hi - r_hi_b.astype(F32)).astype(BF16)
    P = dict(
        w_in=w_in, gla_wa2=gla_wa2, gla_ba=gla_ba, gla_onorm=gla_onorm, lb_all=lb_all, hg_onorm=hg_onorm,
        rg_conv_w=rg_conv_w, rg_conv_b=rg_conv_b, rg_wr=rg_wr, rg_br=rg_br, rg_wi=rg_wi, rg_bi=rg_bi,
        rg_lambda=rg_lambda, w_out=w_out, xa_wq=xa_wq, xa_wo=xa_wo, norm_mix=norm_mix, norm_xattn=norm_xattn,
        norm_ffn=norm_ffn,
        router=[(r_hi_b[j], r_lo_b[j]) for j in range(moe_router.shape[0])],
        ffd_w1=ffd_w1.astype(BF16), ffd_w3=ffd_w3.astype(BF16), ffd_w2=ffd_w2.astype(BF16),
        moe_w1=moe_w1.astype(BF16), moe_w3=moe_w3.astype(BF16), moe_w2=moe_w2.astype(BF16),
        g_final=norm_final.reshape(1, D_MODEL).astype(F32),
    )
    LW = [_layer_weights(l, P) for l in range(depth)]

    mem_flat = mem_prompt.reshape(batch * MEM_LEN, D_MODEL)
    mk, mv = _mem_kv(mem_flat, norm_mem.reshape(depth, 1, D_MODEL).astype(F32), xa_wk.astype(BF16),
                     xa_wv.astype(BF16), 512)
    mem_k_p = mk.reshape(depth, batch, MEM_LEN, XA_HEADS, XA_DH)
    mem_v_p = mv.reshape(depth, batch, MEM_LEN, XA_HEADS, XA_DH)
    dt = x_prompt.dtype
    z_gla = jnp.zeros((depth, batch, GLA_HEADS, GLA_DK, GLA_DV), dt)
    z_hg = jnp.zeros((depth, batch, HG_HEADS, HG_DK, HG_DV), dt)
    z_rg = jnp.zeros((depth, batch, RG_WIDTH), dt)
    z_cv = jnp.zeros((depth, batch, RG_CONV - 1, RG_WIDTH), dt)
    y_prompt, gla_p, hgrn_p, rglru_p, conv_p = _trunk(
        x_prompt, mk.reshape(depth * batch, MEM_LEN, D_MODEL), mv.reshape(depth * batch, MEM_LEN, D_MODEL),
        z_gla, z_hg, z_rg, z_cv, P, LW)
    dec_batch = x_sample.shape[0]
    y_sample, gla_s, hgrn_s, rglru_s, conv_s = _trunk(
        x_sample, cache_mem_k.reshape(depth * dec_batch, MEM_LEN, D_MODEL),
        cache_mem_v.reshape(depth * dec_batch, MEM_LEN, D_MODEL), state_gla, state_hgrn, state_rglru, state_conv,
        P, LW)
    return (y_prompt, y_sample, gla_p, hgrn_p, rglru_p, conv_p, mem_k_p, mem_v_p, gla_s, hgrn_s, rglru_s, conv_s)
```

```python
import functools

import jax
import jax.numpy as jnp
from jax import lax
from jax.experimental import pallas as pl
from jax.experimental.pallas import tpu as pltpu

F32 = jnp.float32
BF16 = jnp.bfloat16

D_MODEL = 1024
GLA_HEADS = 4
GLA_DK = 48
GLA_DV = 96
GLA_RANK = 16
GLA_TAU = 16.0
HG_HEADS = 4
HG_DK = 64
HG_DV = 96
RG_WIDTH = 256
RG_BLOCKS = 4
RG_CONV = 4
RG_C = 8.0
MEM_LEN = 256
XA_HEADS = 4
XA_DH = D_MODEL // XA_HEADS
D_FF = 2816
N_EXPERTS = 8
EPS = 1e-6
IN_SIZES = (192, 192, 384, 384, 16, 256, 256, 384, 384, 256, 256)

LANE = 128
SUBLANE = 8
VMEM_LIMIT_BYTES = 56 * 1024 * 1024

HEADS = 4
HEAD_K = 64
KL = HEADS * HEAD_K
HEAD_V = 96
VL = HEADS * HEAD_V
SUB = 16
GA_PAD = LANE

OFF_GQ = 0
OFF_GK = OFF_GQ + KL
OFF_GA = OFF_GK + KL
OFF_GV = OFF_GA + GA_PAD
OFF_GR = OFF_GV + VL
OFF_HQ = OFF_GR + VL
OFF_HF = OFF_HQ + KL
OFF_HI = OFF_HF + KL
OFF_HG = OFF_HI + VL
OFF_RX = OFF_HG + VL
OFF_RG = OFF_RX + RG_WIDTH
PROJ_W = OFF_RG + RG_WIDTH


def _cparams(*sem):
    return pltpu.CompilerParams(dimension_semantics=sem, vmem_limit_bytes=VMEM_LIMIT_BYTES)


def _dot(a, b):
    return jnp.dot(a, b, preferred_element_type=F32)


def _dot_nt(a, b):
    return lax.dot_general(a, b, (((1,), (1,)), ((), ())), preferred_element_type=F32)


def _dot_tn(a, b):
    return lax.dot_general(a, b, (((0,), (0,)), ((), ())), preferred_element_type=F32)


def _sigmoid(x):
    return 1.0 / (1.0 + jnp.exp(-x))


def _silu(x):
    return x * _sigmoid(x)


def _softplus(x):
    return jnp.maximum(x, 0.0) + jnp.log1p(jnp.exp(-jnp.abs(x)))


def _expm1(x):
    return jnp.tanh(0.5 * x) * (jnp.exp(x) + 1.0)


def _rms(x, g):
    return x * lax.rsqrt(jnp.mean(x * x, axis=-1, keepdims=True) + EPS) * g


def _split3(x):
    hi = x.astype(BF16)
    r1 = x - hi.astype(F32)
    mid = r1.astype(BF16)
    lo = (r1 - mid.astype(F32)).astype(BF16)
    return hi, mid, lo


def _proj_body(x_ref, g_ref, w_ref, o_ref):
    h = _rms(x_ref[...], g_ref[...]).astype(BF16)
    o_ref[...] = _dot(h, w_ref[...])


def _proj(x, g, w, tm):
    m, d = x.shape
    n = w.shape[1]
    return pl.pallas_call(
        _proj_body,
        grid=(m // tm,),
        in_specs=[
            pl.BlockSpec((tm, d), lambda i: (i, 0)),
            pl.BlockSpec((1, d), lambda i: (0, 0)),
            pl.BlockSpec((d, n), lambda i: (0, 0)),
        ],
        out_specs=pl.BlockSpec((tm, n), lambda i: (i, 0)),
        out_shape=jax.ShapeDtypeStruct((m, n), F32),
        compiler_params=_cparams("parallel"),
        name="in_proj",
    )(x, g, w)


def _memkv_body(m_ref, g_ref, wk_ref, wv_ref, k_ref, v_ref):
    h = _rms(m_ref[...], g_ref[...]).astype(BF16)
    k_ref[...] = _dot(h, wk_ref[...])
    v_ref[...] = _dot(h, wv_ref[...])


def _mem_kv(mem, g, wk, wv, tm):
    m, d = mem.shape
    nl = g.shape[0]
    out = jax.ShapeDtypeStruct((nl, m, d), F32)
    return pl.pallas_call(
        _memkv_body,
        grid=(nl, m // tm),
        in_specs=[
            pl.BlockSpec((tm, d), lambda l, i: (i, 0)),
            pl.BlockSpec((None, 1, d), lambda l, i: (l, 0, 0)),
            pl.BlockSpec((None, d, d), lambda l, i: (l, 0, 0)),
            pl.BlockSpec((None, d, d), lambda l, i: (l, 0, 0)),
        ],
        out_specs=[
            pl.BlockSpec((None, tm, d), lambda l, i: (l, i, 0)),
            pl.BlockSpec((None, tm, d), lambda l, i: (l, i, 0)),
        ],
        out_shape=[out, out],
        compiler_params=_cparams("parallel", "parallel"),
        name="mem_kv",
    )(mem, g, wk, wv)


def _key_head(shape, dim):
    return lax.shift_right_logical(lax.broadcasted_iota(jnp.int32, shape, dim), 6)


def _val_head(shape, dim):
    i = lax.broadcasted_iota(jnp.int32, shape, dim)
    one = jnp.ones(shape, jnp.int32)
    zero = jnp.zeros(shape, jnp.int32)
    return (jnp.where(i >= HEAD_V, one, zero) + jnp.where(i >= 2 * HEAD_V, one, zero)
            + jnp.where(i >= 3 * HEAD_V, one, zero))


def _gated_chunk(q, k, v, lg, st_ref, c):
    nsub = c // SUB
    row = lax.broadcasted_iota(jnp.int32, (c, c), 0)
    col = lax.broadcasted_iota(jnp.int32, (c, c), 1)
    ltri = jnp.where(col <= row, 1.0, 0.0).astype(BF16)
    hi, mid, lo = _split3(lg)
    b = _dot(ltri, hi) + _dot(ltri, mid) + _dot(ltri, lo)
    b_last = b[c - 1:c, :]

    s_t = st_ref[...]
    o_inter = _dot_nt((q * jnp.exp(b)).astype(BF16), s_t.astype(BF16))
    k_end = (k * jnp.exp(b_last - b)).astype(BF16)
    upd = _dot_tn(v.astype(BF16), k_end)
    same_head = _val_head((VL, KL), 0) == _key_head((VL, KL), 1)
    st_ref[...] = s_t * jnp.exp(b_last) + jnp.where(same_head, upd, 0.0)

    khead = _key_head((SUB, KL), 1)
    vhead = _val_head((SUB, VL), 1)
    rr = lax.broadcasted_iota(jnp.int32, (SUB, HEADS * SUB), 0)
    cc = lax.broadcasted_iota(jnp.int32, (SUB, HEADS * SUB), 1)
    causal = jnp.bitwise_and(cc, SUB - 1) <= rr
    pieces = [o_inter[i * SUB:(i + 1) * SUB, :] for i in range(nsub)]
    for j in range(nsub):
        r0 = j * SUB
        e_j = b[r0 + SUB - 1:r0 + SUB, :]
        e_prev = b[r0 - 1:r0, :] if j else jnp.zeros_like(e_j)
        m = 0.5 * (e_prev + e_j)
        lhs = (q[r0:, :] * jnp.exp(b[r0:, :] - m)).astype(BF16)
        ku = k[r0:r0 + SUB, :] * jnp.exp(m - b[r0:r0 + SUB, :])
        ks = jnp.concatenate([jnp.where(khead == h, ku, 0.0) for h in range(HEADS)], axis=0).astype(BF16)
        p = _dot_nt(lhs, ks)
        vj = v[r0:r0 + SUB, :]
        vs = jnp.concatenate([jnp.where(vhead == h, vj, 0.0) for h in range(HEADS)], axis=0).astype(BF16)
        p_diag = jnp.where(causal, p[:SUB, :], 0.0)
        if c - r0 > SUB:
            p = jnp.concatenate([p_diag, p[SUB:, :]], axis=0)
        else:
            p = p_diag
        pv = _dot(p.astype(BF16), vs)
        for i in range(j, nsub):
            pieces[i] = pieces[i] + pv[(i - j) * SUB:(i - j + 1) * SUB, :]
    return jnp.concatenate(pieces, axis=0) if nsub > 1 else pieces[0]


def _head_rms(o, gain):
    seg = (_val_head((VL, VL), 0) == _val_head((VL, VL), 1))
    ones = jnp.where(seg, 1.0, 0.0).astype(BF16)
    sq = o * o
    hi = sq.astype(BF16)
    lo = (sq - hi.astype(F32)).astype(BF16)
    ms = (_dot(hi, ones) + _dot(lo, ones)) * (1.0 / HEAD_V)
    return o * lax.rsqrt(ms + EPS) * gain


def _mixer_body(p_ref, wa2_ref, ba_ref, gon_ref, lb_ref, hon_ref, cw_ref, cb_ref, wr_ref, br_ref, wi_ref,
                bi_ref, lam_ref, sg0_ref, sh0_ref, hr0_ref, cv0_ref,
                mix_ref, sg_ref, sh_ref, hr_ref, cv_ref,
                sg_t, sh_t, h_car, xc, *, c):
    t = pl.program_id(1)
    nt = pl.num_programs(1)
    npad = SUBLANE
    ncar = RG_CONV - 1

    @pl.when(t == 0)
    def _():
        sg_t[...] = sg0_ref[...]
        sh_t[...] = sh0_ref[...]
        h_car[...] = hr0_ref[...]
        xc[npad - ncar:npad, :] = cv0_ref[...]

    gq = p_ref[:, OFF_GQ:OFF_GQ + KL] * (GLA_DK ** -0.5)
    gk = p_ref[:, OFF_GK:OFF_GK + KL]
    ga = p_ref[:, OFF_GA:OFF_GA + GA_PAD]
    z = _dot(ga.astype(BF16), wa2_ref[...]) + ba_ref[...]
    lg = -_softplus(-z) * (1.0 / GLA_TAU)
    o = _gated_chunk(gq, gk, p_ref[:, OFF_GV:OFF_GV + VL], lg, sg_t, c)
    o = _head_rms(o, gon_ref[...]) * _silu(p_ref[:, OFF_GR:OFF_GR + VL])
    mix_ref[:, 0:VL] = o.astype(BF16)

    z = p_ref[:, OFF_HF:OFF_HF + KL]
    lb = lb_ref[...]
    f = lb + (1.0 - lb) * _sigmoid(z)
    hk = (1.0 - lb) * _sigmoid(-z)
    hq = _silu(p_ref[:, OFF_HQ:OFF_HQ + KL])
    o = _gated_chunk(hq, hk, p_ref[:, OFF_HI:OFF_HI + VL], jnp.log(f), sh_t, c)
    o = _head_rms(o, hon_ref[...]) * _silu(p_ref[:, OFF_HG:OFF_HG + VL])
    mix_ref[:, VL:2 * VL] = o.astype(BF16)

    xb = p_ref[:, OFF_RX:OFF_RX + RG_WIDTH]
    xc[npad:npad + c, :] = xb
    u = cb_ref[...]
    for j in range(RG_CONV):
        u = u + xc[npad - ncar + j:npad - ncar + j + c, :] * cw_ref[j:j + 1, :]
    xc[npad - ncar:npad, :] = xc[npad + c - ncar:npad + c, :]
    ub = u.astype(BF16)
    r = _sigmoid(_dot(ub, wr_ref[...]) + br_ref[...])
    ig = _sigmoid(_dot(ub, wi_ref[...]) + bi_ref[...])
    log_a = (-RG_C * r) * _softplus(-lam_ref[...])
    a = jnp.exp(log_a)
    beta = jnp.sqrt(jnp.maximum(-_expm1(2.0 * log_a), 1e-12))
    bt = beta * (ig * u)
    rows = lax.broadcasted_iota(jnp.int32, (c, RG_WIDTH), 0)
    d = 1
    while d < c:
        keep = rows >= d
        a_sh = jnp.where(keep, pltpu.roll(a, d, 0), 1.0)
        b_sh = jnp.where(keep, pltpu.roll(bt, d, 0), 0.0)
        bt = a * b_sh + bt
        a = a * a_sh
        d *= 2
    h = bt + a * h_car[...]
    h_car[...] = h[c - 1:c, :]
    gb = p_ref[:, OFF_RG:OFF_RG + RG_WIDTH]
    gelu = gb * (0.5 * (1.0 + jnp.tanh(0.7978845608028654 * (gb + 0.044715 * (gb * gb * gb)))))
    mix_ref[:, 2 * VL:2 * VL + RG_WIDTH] = (h * gelu).astype(BF16)

    @pl.when(t == nt - 1)
    def _():
        sg_ref[...] = sg_t[...]
        sh_ref[...] = sh_t[...]
        hr_ref[...] = h_car[...]
        cv_ref[...] = xc[npad - ncar:npad, :]


def _mixer(p, lw, sg0, sh0, hr0, cv0, batch, seq, c):
    nt = seq // c
    vec = lambda n: pl.BlockSpec((1, n), lambda b, t: (0, 0))
    mat = lambda r, n: pl.BlockSpec((r, n), lambda b, t: (0, 0))
    st = lambda r, n: pl.BlockSpec((None, r, n), lambda b, t: (b, 0, 0))
    in_specs = [
        pl.BlockSpec((c, PROJ_W), lambda b, t: (b * nt + t, 0)),
        mat(GA_PAD, KL), vec(KL), vec(VL), vec(KL), vec(VL), mat(SUBLANE, RG_WIDTH), vec(RG_WIDTH),
        mat(RG_WIDTH, RG_WIDTH), vec(RG_WIDTH), mat(RG_WIDTH, RG_WIDTH), vec(RG_WIDTH), vec(RG_WIDTH),
        st(VL, KL), st(VL, KL), st(1, RG_WIDTH), st(RG_CONV - 1, RG_WIDTH),
    ]
    out_specs = [
        pl.BlockSpec((c, D_MODEL), lambda b, t: (b * nt + t, 0)),
        st(VL, KL), st(VL, KL), st(1, RG_WIDTH), st(RG_CONV - 1, RG_WIDTH),
    ]
    out_shape = [
        jax.ShapeDtypeStruct((batch * seq, D_MODEL), BF16),
        jax.ShapeDtypeStruct((batch, VL, KL), F32),
        jax.ShapeDtypeStruct((batch, VL, KL), F32),
        jax.ShapeDtypeStruct((batch, 1, RG_WIDTH), F32),
        jax.ShapeDtypeStruct((batch, RG_CONV - 1, RG_WIDTH), F32),
    ]
    scratch = [
        pltpu.VMEM((VL, KL), F32), pltpu.VMEM((VL, KL), F32), pltpu.VMEM((1, RG_WIDTH), F32),
        pltpu.VMEM((c + SUBLANE, RG_WIDTH), F32),
    ]
    return pl.pallas_call(
        functools.partial(_mixer_body, c=c),
        grid=(batch, nt),
        in_specs=in_specs,
        out_specs=out_specs,
        out_shape=out_shape,
        scratch_shapes=scratch,
        compiler_params=_cparams("parallel", "arbitrary"),
        name="mixer",
    )(p, lw["wa2"], lw["ba"], lw["gla_on"], lw["lb"], lw["hg_on"], lw["conv_w"], lw["conv_b"], lw["wr"],
      lw["br"], lw["wi"], lw["bi"], lw["lam"], sg0, sh0, hr0, cv0)


ROUTE_I1, ROUTE_I2, ROUTE_G1, ROUTE_G2, ROUTE_R1, ROUTE_R2 = range(6)


def _attn_body(*refs, with_router):
    if with_router:
        (x_ref, mix_ref, wout_ref, gx_ref, wq_ref, k_ref, v_ref, wo_ref, gf_ref, rhi_ref, rlo_ref,
         x2_ref, hb_ref, route_ref, cnt_ref, run_cnt) = refs
    else:
        x_ref, mix_ref, wout_ref, gx_ref, wq_ref, k_ref, v_ref, wo_ref, gf_ref, x2_ref, hb_ref = refs
    x1 = x_ref[...] + _dot(mix_ref[...], wout_ref[...])
    h = _rms(x1, gx_ref[...]).astype(BF16)
    q = _dot(h, wq_ref[...])
    outs = []
    for hd in range(XA_HEADS):
        sl = slice(hd * XA_DH, (hd + 1) * XA_DH)
        s = _dot_nt(q[:, sl].astype(BF16), k_ref[:, sl].astype(BF16)) * (XA_DH ** -0.5)
        e = jnp.exp(s - jnp.max(s, axis=-1, keepdims=True))
        p = e / jnp.sum(e, axis=-1, keepdims=True)
        outs.append(_dot(p.astype(BF16), v_ref[:, sl].astype(BF16)))
    o = jnp.concatenate(outs, axis=-1).astype(BF16)
    x2 = x1 + _dot(o, wo_ref[...])
    x2_ref[...] = x2
    h2 = _rms(x2, gf_ref[...])
    h2_hi = h2.astype(BF16)
    if not with_router:
        hb_ref[...] = h2_hi
        return
    hb_ref[...] = h2

    @pl.when((pl.program_id(0) == 0) & (pl.program_id(1) == 0))
    def _():
        run_cnt[...] = jnp.zeros_like(run_cnt)

    h2_lo = (h2 - h2_hi.astype(F32)).astype(BF16)
    logits = _dot(h2_hi, rhi_ref[...]) + _dot(h2_lo, rhi_ref[...]) + _dot(h2_hi, rlo_ref[...])
    tq = logits.shape[0]
    lane = lax.broadcasted_iota(jnp.int32, logits.shape, 1).astype(F32)
    neg = -jnp.inf
    lg = jnp.where(lane < N_EXPERTS, logits, neg)
    m1 = jnp.max(lg, axis=-1, keepdims=True)
    i1 = jnp.min(jnp.where(lg == m1, lane, float(LANE)), axis=-1, keepdims=True)
    lg2 = jnp.where(lane == i1, neg, lg)
    m2 = jnp.max(lg2, axis=-1, keepdims=True)
    i2 = jnp.min(jnp.where(lg2 == m2, lane, float(LANE)), axis=-1, keepdims=True)
    e2 = jnp.exp(m2 - m1)
    g1 = 1.0 / (1.0 + e2)
    g2 = e2 / (1.0 + e2)
    sel = jnp.maximum(jnp.where(lane == i1, 1.0, 0.0), jnp.where(lane == i2, 1.0, 0.0))
    rr = lax.broadcasted_iota(jnp.int32, (tq, tq), 0)
    cc = lax.broadcasted_iota(jnp.int32, (tq, tq), 1)
    before = jnp.where(cc < rr, 1.0, 0.0).astype(BF16)
    rank = _dot(before, sel.astype(BF16)) + run_cnt[...]
    r1 = jnp.sum(jnp.where(lane == i1, rank, 0.0), axis=-1, keepdims=True)
    r2 = jnp.sum(jnp.where(lane == i2, rank, 0.0), axis=-1, keepdims=True)
    run_cnt[...] = run_cnt[...] + jnp.sum(sel, axis=0, keepdims=True)
    cnt_ref[...] = run_cnt[...]
    rec = jnp.zeros_like(logits)
    for ln, val in ((ROUTE_I1, i1), (ROUTE_I2, i2), (ROUTE_G1, g1), (ROUTE_G2, g2), (ROUTE_R1, r1), (ROUTE_R2, r2)):
        rec = jnp.where(lane == float(ln), val, rec)
    route_ref[...] = rec


def _attn(x, mix, mk, mv, layer, lw, g_ffn, router, batch, seq, tq):
    m = batch * seq
    nt = seq // tq
    with_router = router is not None
    row = pl.BlockSpec((tq, D_MODEL), lambda b, t: (b * nt + t, 0))
    vec = pl.BlockSpec((1, D_MODEL), lambda b, t: (0, 0))
    mat = pl.BlockSpec((D_MODEL, D_MODEL), lambda b, t: (0, 0))
    kv = pl.BlockSpec((None, MEM_LEN, D_MODEL), lambda b, t: (layer * batch + b, 0, 0))
    in_specs = [row, row, mat, vec, mat, kv, kv, mat, vec]
    args = [x, mix, lw["w_out"], lw["g_xattn"], lw["wq"], mk, mv, lw["wo"], g_ffn]
    out_specs = [row, row]
    out_shape = [jax.ShapeDtypeStruct((m, D_MODEL), F32),
                 jax.ShapeDtypeStruct((m, D_MODEL), F32 if with_router else BF16)]
    scratch = []
    if with_router:
        rspec = pl.BlockSpec((D_MODEL, LANE), lambda b, t: (0, 0))
        in_specs += [rspec, rspec]
        args += [router[0], router[1]]
        out_specs += [pl.BlockSpec((tq, LANE), lambda b, t: (b * nt + t, 0)),
                      pl.BlockSpec((1, LANE), lambda b, t: (0, 0))]
        out_shape += [jax.ShapeDtypeStruct((m, LANE), F32), jax.ShapeDtypeStruct((1, LANE), F32)]
        scratch = [pltpu.VMEM((1, LANE), F32)]
    sem = ("arbitrary", "arbitrary") if with_router else ("parallel", "parallel")
    return pl.pallas_call(
        functools.partial(_attn_body, with_router=with_router),
        grid=(batch, nt),
        in_specs=in_specs,
        out_specs=out_specs,
        out_shape=out_shape,
        scratch_shapes=scratch,
        compiler_params=_cparams(*sem),
        name="xattn_router" if with_router else "xattn",
    )(*args)


FF_CHUNK = 256


def _swiglu_tile(h, w1_ref, w3_ref, w2_ref):
    y = None
    for f0 in range(0, D_FF, FF_CHUNK):
        a = _dot(h, w1_ref[:, f0:f0 + FF_CHUNK])
        c = _dot(h, w3_ref[:, f0:f0 + FF_CHUNK])
        g = (_silu(a) * c).astype(BF16)
        part = _dot(g, w2_ref[f0:f0 + FF_CHUNK, :])
        y = part if y is None else y + part
    return y


def _ffn_body(hb_ref, x_ref, w1_ref, w3_ref, w2_ref, gfin_ref, o_ref, *, final_norm):
    out = x_ref[...] + _swiglu_tile(hb_ref[...], w1_ref, w3_ref, w2_ref)
    o_ref[...] = _rms(out, gfin_ref[...]) if final_norm else out


def _ffn(hb, x, w1, w3, w2, g_final, tm, final_norm):
    m = x.shape[0]
    row = pl.BlockSpec((tm, D_MODEL), lambda i: (i, 0))
    return pl.pallas_call(
        functools.partial(_ffn_body, final_norm=final_norm),
        grid=(m // tm,),
        in_specs=[
            row, row,
            pl.BlockSpec((D_MODEL, D_FF), lambda i: (0, 0)),
            pl.BlockSpec((D_MODEL, D_FF), lambda i: (0, 0)),
            pl.BlockSpec((D_FF, D_MODEL), lambda i: (0, 0)),
            pl.BlockSpec((1, D_MODEL), lambda i: (0, 0)),
        ],
        out_specs=row,
        out_shape=jax.ShapeDtypeStruct((m, D_MODEL), F32),
        compiler_params=_cparams("parallel"),
        name="ffn",
    )(hb, x, w1, w3, w2, g_final)


def _route_plan(route, cnt, m, tm):
    i1 = route[:, ROUTE_I1].astype(jnp.int32)
    i2 = route[:, ROUTE_I2].astype(jnp.int32)
    r1 = route[:, ROUTE_R1].astype(jnp.int32)
    r2 = route[:, ROUTE_R2].astype(jnp.int32)
    counts = cnt[0, :N_EXPERTS].astype(jnp.int32)
    padded = ((counts + tm - 1) // tm) * tm
    off_end = jnp.cumsum(padded)
    off = off_end - padded
    pos1 = jnp.take(off, i1) + r1
    pos2 = jnp.take(off, i2) + r2
    n_tiles = (2 * m) // tm + N_EXPERTS
    n_used = off_end[-1] // tm
    tile = jnp.minimum(jnp.arange(n_tiles, dtype=jnp.int32), n_used - 1)
    tile_expert = jnp.sum((tile[:, None] * tm >= off_end[None, :]).astype(jnp.int32), axis=1)
    return dict(pos1=pos1, pos2=pos2, tile_expert=tile_expert, n_used=n_used.reshape(1),
                pad_start=off + counts, pad_len=padded - counts, n_tiles=n_tiles, n_slots=n_tiles * tm)


def _dispatch_body(pad_start, pad_len, n_used, pos1_ref, pos2_ref, h_ref, xs_hbm, zrow, sem, zsem, *, tb, n_tiles):
    i = pl.program_id(0)

    def row_copy(r, p):
        return pltpu.make_async_copy(h_ref.at[pl.ds(r, 1)], xs_hbm.at[pl.ds(p, 1)], sem)

    def issue(r, c):
        row_copy(r, pos1_ref[r]).start(priority=0)
        row_copy(r, pos2_ref[r]).start(priority=1)
        return c

    lax.fori_loop(0, tb, issue, 0)

    @pl.when(i == 0)
    def _():
        zrow[...] = jnp.zeros_like(zrow)

        def zero_copy(p):
            return pltpu.make_async_copy(zrow.at[pl.ds(0, 1)], xs_hbm.at[pl.ds(p, 1)], zsem)

        def zero_tile(t):
            return pltpu.make_async_copy(zrow, xs_hbm.at[pl.ds(t * tb, tb)], zsem)

        for e in range(N_EXPERTS):
            lax.fori_loop(0, pad_len[e], lambda r, c: (zero_copy(pad_start[e] + r).start(), c)[1], 0)
        lax.fori_loop(n_used[0], n_tiles, lambda t, c: (zero_tile(t).start(), c)[1], 0)
        for e in range(N_EXPERTS):
            lax.fori_loop(0, pad_len[e], lambda r, c: (zero_copy(pad_start[e] + r).wait(), c)[1], 0)
        lax.fori_loop(n_used[0], n_tiles, lambda t, c: (zero_tile(t).wait(), c)[1], 0)

    pltpu.make_async_copy(h_ref, xs_hbm.at[pl.ds(0, tb)], sem).wait()
    pltpu.make_async_copy(h_ref, xs_hbm.at[pl.ds(0, tb)], sem).wait()


def _dispatch(h2, plan, tb):
    m = h2.shape[0]
    ns = plan["n_slots"]
    idx = pl.BlockSpec((tb,), lambda i, *_: (i,), memory_space=pltpu.SMEM)
    return pl.pallas_call(
        functools.partial(_dispatch_body, tb=tb, n_tiles=plan["n_tiles"]),
        grid_spec=pltpu.PrefetchScalarGridSpec(
            num_scalar_prefetch=3,
            grid=(m // tb,),
            in_specs=[idx, idx, pl.BlockSpec((tb, D_MODEL), lambda i, *_: (i, 0))],
            out_specs=pl.BlockSpec(memory_space=pl.ANY),
            scratch_shapes=[pltpu.VMEM((tb, D_MODEL), F32), pltpu.SemaphoreType.DMA(()),
                            pltpu.SemaphoreType.DMA(())],
        ),
        out_shape=jax.ShapeDtypeStruct((ns, D_MODEL), F32),
        compiler_params=_cparams("arbitrary"),
        name="moe_dispatch",
    )(plan["pad_start"], plan["pad_len"], plan["n_used"], plan["pos1"], plan["pos2"], h2)


def _expert_body(tile_expert, n_used, xs_ref, w1_ref, w3_ref, w2_ref, ys_ref):
    used = pl.program_id(0) < n_used[0]

    @pl.when(used)
    def _():
        ys_ref[...] = _swiglu_tile(xs_ref[...].astype(BF16), w1_ref, w3_ref, w2_ref)

    @pl.when(jnp.logical_not(used))
    def _():
        ys_ref[...] = jnp.zeros_like(ys_ref)


def _expert_ffn(xs, plan, w1, w3, w2, tm):
    ns = xs.shape[0]
    row = pl.BlockSpec((tm, D_MODEL), lambda i, te, nu: (jnp.minimum(i, nu[0] - 1), 0))
    out_row = pl.BlockSpec((tm, D_MODEL), lambda i, te, nu: (i, 0))
    return pl.pallas_call(
        _expert_body,
        grid_spec=pltpu.PrefetchScalarGridSpec(
            num_scalar_prefetch=2,
            grid=(plan["n_tiles"],),
            in_specs=[
                row,
                pl.BlockSpec((None, D_MODEL, D_FF), lambda i, te, nu: (te[i], 0, 0)),
                pl.BlockSpec((None, D_MODEL, D_FF), lambda i, te, nu: (te[i], 0, 0)),
                pl.BlockSpec((None, D_FF, D_MODEL), lambda i, te, nu: (te[i], 0, 0)),
            ],
            out_specs=out_row,
        ),
        out_shape=jax.ShapeDtypeStruct((ns, D_MODEL), F32),
        compiler_params=_cparams("arbitrary"),
        name="moe_experts",
    )(plan["tile_expert"], plan["n_used"], xs, w1, w3, w2)


def _combine_body(pos1_ref, pos2_ref, route_ref, x_ref, gfin_ref, ys_hbm, o_ref, ya, yb, sem, *, tb, final_norm):
    def issue(r, c):
        pltpu.make_async_copy(ys_hbm.at[pl.ds(pos1_ref[r], 1)], ya.at[pl.ds(r, 1)], sem).start()
        pltpu.make_async_copy(ys_hbm.at[pl.ds(pos2_ref[r], 1)], yb.at[pl.ds(r, 1)], sem).start()
        return c

    lax.fori_loop(0, tb, issue, 0)
    pltpu.make_async_copy(ys_hbm.at[pl.ds(0, tb)], ya, sem).wait()
    pltpu.make_async_copy(ys_hbm.at[pl.ds(0, tb)], yb, sem).wait()
    rec = route_ref[...]
    lane = lax.broadcasted_iota(jnp.int32, rec.shape, 1)
    g1 = jnp.sum(jnp.where(lane == ROUTE_G1, rec, 0.0), axis=-1, keepdims=True)
    g2 = jnp.sum(jnp.where(lane == ROUTE_G2, rec, 0.0), axis=-1, keepdims=True)
    out = x_ref[...] + (g1 * ya[...] + g2 * yb[...])
    o_ref[...] = _rms(out, gfin_ref[...]) if final_norm else out


def _combine(x, ys, route, plan, g_final, tb, final_norm):
    m = x.shape[0]
    idx = pl.BlockSpec((tb,), lambda i: (i,), memory_space=pltpu.SMEM)
    row = pl.BlockSpec((tb, D_MODEL), lambda i: (i, 0))
    return pl.pallas_call(
        functools.partial(_combine_body, tb=tb, final_norm=final_norm),
        grid=(m // tb,),
        in_specs=[idx, idx, pl.BlockSpec((tb, LANE), lambda i: (i, 0)), row,
                  pl.BlockSpec((1, D_MODEL), lambda i: (0, 0)), pl.BlockSpec(memory_space=pl.ANY)],
        out_specs=row,
        out_shape=jax.ShapeDtypeStruct((m, D_MODEL), F32),
        scratch_shapes=[pltpu.VMEM((tb, D_MODEL), F32), pltpu.VMEM((tb, D_MODEL), F32),
                        pltpu.SemaphoreType.DMA(())],
        compiler_params=_cparams("arbitrary"),
        name="moe_combine",
    )(plan["pos1"], plan["pos2"], route, x, g_final, ys)


def _moe(h2, x, route, cnt, w1, w3, w2, g_final, tm, final_norm):
    m = x.shape[0]
    plan = _route_plan(route, cnt, m, tm)
    xs = _dispatch(h2, plan, tm)
    ys = _expert_ffn(xs, plan, w1, w3, w2, tm)
    return _combine(x, ys, route, plan, g_final, tm, final_norm)


def _pad_heads(w, dk):
    lead = w.shape[:-1]
    w = w.reshape(lead + (HEADS, dk))
    w = jnp.pad(w, [(0, 0)] * len(lead) + [(0, 0), (0, HEAD_K - dk)])
    return w.reshape(lead + (KL,))


def _block_diag(w):
    nb, n, _ = w.shape
    eye = jnp.eye(nb, dtype=w.dtype)
    return jnp.einsum("ncd,nm->ncmd", w, eye).reshape(nb * n, nb * n)


def _layer_weights(l, P):
    offs = [0]
    for s in IN_SIZES:
        offs.append(offs[-1] + s)
    cols = [P["w_in"][l][:, offs[i]:offs[i + 1]] for i in range(len(IN_SIZES))]
    gq, gk, gv, gr, ga, hq, hf, hi, hg, rx, rg = cols
    w_in = jnp.concatenate([
        _pad_heads(gq, GLA_DK), _pad_heads(gk, GLA_DK), jnp.pad(ga, ((0, 0), (0, GA_PAD - GLA_RANK))), gv, gr,
        hq, hf, hi, hg, rx, rg], axis=1).astype(BF16)
    row = lambda v: v.reshape(1, -1).astype(F32)
    return dict(
        g_mix=row(P["norm_mix"][l]), w_in=w_in,
        wa2=jnp.pad(_pad_heads(P["gla_wa2"][l], GLA_DK), ((0, GA_PAD - GLA_RANK), (0, 0))).astype(BF16),
        ba=row(_pad_heads(P["gla_ba"][l], GLA_DK)), gla_on=row(P["gla_onorm"][l]),
        lb=row(P["lb_all"][l]), hg_on=row(P["hg_onorm"][l]),
        conv_w=jnp.pad(P["rg_conv_w"][l], ((0, SUBLANE - RG_CONV), (0, 0))).astype(F32),
        conv_b=row(P["rg_conv_b"][l]),
        wr=_block_diag(P["rg_wr"][l]).astype(BF16), br=row(P["rg_br"][l]),
        wi=_block_diag(P["rg_wi"][l]).astype(BF16), bi=row(P["rg_bi"][l]),
        lam=row(P["rg_lambda"][l]),
        w_out=P["w_out"][l].astype(BF16), g_xattn=row(P["norm_xattn"][l]),
        wq=P["xa_wq"][l].astype(BF16), wo=P["xa_wo"][l].astype(BF16),
        g_ffn=row(P["norm_ffn"][l]),
    )


def _state_to_kernel(s, dk):
    s = jnp.pad(s, ((0, 0), (0, 0), (0, HEAD_K - dk), (0, 0)))
    eye = jnp.eye(HEADS, dtype=s.dtype)
    b = s.shape[0]
    return jnp.einsum("bhkv,hg->bhvgk", s, eye).reshape(b, VL, KL)


def _state_from_kernel(st, dk):
    b = st.shape[0]
    s = st.reshape(b, HEADS, HEAD_V, HEADS, HEAD_K)
    s = jnp.stack([s[:, h, :, h, :] for h in range(HEADS)], axis=1)
    return jnp.swapaxes(s, -1, -2)[:, :, :dk, :]


def _trunk(x, mem_k, mem_v, s_gla, s_hg, s_rg, s_conv, P, LW):
    batch, seq, _ = x.shape
    depth = len(LW)
    m = batch * seq
    c = min(seq, 128)
    tm = min(m, 512)
    tq = min(seq, 512)
    xf = x.reshape(m, D_MODEL)
    n_gla, n_hg, n_rg, n_cv = [], [], [], []
    for l in range(depth):
        lw = LW[l]
        p = _proj(xf, lw["g_mix"], lw["w_in"], tm)
        if s_gla is None:
            sg0 = sh0 = jnp.zeros((batch, VL, KL), F32)
            hr0 = jnp.zeros((batch, 1, RG_WIDTH), F32)
            cv0 = jnp.zeros((batch, RG_CONV - 1, RG_WIDTH), F32)
        else:
            sg0, sh0 = _state_to_kernel(s_gla[l], GLA_DK), _state_to_kernel(s_hg[l], HG_DK)
            hr0, cv0 = s_rg[l].reshape(batch, 1, RG_WIDTH), s_conv[l]
        mix, sg, sh, hr, cv = _mixer(p, lw, sg0, sh0, hr0, cv0, batch, seq, c)
        moe = l % 2 == 1
        j = l // 2
        router = P["router"][j] if moe else None
        res = _attn(xf, mix, mem_k, mem_v, l, lw, lw["g_ffn"], router, batch, seq, tq)
        last = l == depth - 1
        if moe:
            x2, h2, route, cnt = res
            xf = _moe(h2, x2, route, cnt, P["moe_w1"][j], P["moe_w3"][j], P["moe_w2"][j], P["g_final"], tm, last)
        else:
            x2, hb = res
            xf = _ffn(hb, x2, P["ffd_w1"][j], P["ffd_w3"][j], P["ffd_w2"][j], P["g_final"], tm, last)
        n_gla.append(_state_from_kernel(sg, GLA_DK))
        n_hg.append(_state_from_kernel(sh, HG_DK))
        n_rg.append(hr.reshape(batch, RG_WIDTH))
        n_cv.append(cv)
    return (xf.reshape(batch, seq, D_MODEL), jnp.stack(n_gla), jnp.stack(n_hg), jnp.stack(n_rg), jnp.stack(n_cv))


def kernel(x_prompt, x_sample, state_gla, state_hgrn, state_rglru, state_conv, cache_mem_k, cache_mem_v, mem_prompt, norm_mix, norm_xattn, norm_ffn, norm_mem, norm_final, w_in, gla_wa2, gla_ba, gla_onorm, hg_lb, hg_onorm, rg_conv_w, rg_conv_b, rg_wr, rg_br, rg_wi, rg_bi, rg_lambda, w_out, xa_wq, xa_wk, xa_wv, xa_wo, ffd_w1, ffd_w3, ffd_w2, moe_router, moe_w1, moe_w3, moe_w2):
    depth = w_in.shape[0]
    batch = x_prompt.shape[0]
    pl_ = jax.nn.softmax(hg_lb.astype(F32), axis=0)
    lb_all = jnp.cumsum(pl_, axis=0) - pl_[:1]
    r_hi = jnp.pad(moe_router, ((0, 0), (0, 0), (0, LANE - N_EXPERTS)))
    r_hi_b = r_hi.astype(BF16)
    r_lo_b = (r_hi - r_hi_b.astype(F32)).astype(BF16)
    P = dict(
        w_in=w_in, gla_wa2=gla_wa2, gla_ba=gla_ba, gla_onorm=gla_onorm, lb_all=lb_all, hg_onorm=hg_onorm,
        rg_conv_w=rg_conv_w, rg_conv_b=rg_conv_b, rg_wr=rg_wr, rg_br=rg_br, rg_wi=rg_wi, rg_bi=rg_bi,
        rg_lambda=rg_lambda, w_out=w_out, xa_wq=xa_wq, xa_wo=xa_wo, norm_mix=norm_mix, norm_xattn=norm_xattn,
        norm_ffn=norm_ffn,
        router=[(r_hi_b[j], r_lo_b[j]) for j in range(moe_router.shape[0])],
        ffd_w1=ffd_w1.astype(BF16), ffd_w3=ffd_w3.astype(BF16), ffd_w2=ffd_w2.astype(BF16),
        moe_w1=moe_w1.astype(BF16), moe_w3=moe_w3.astype(BF16), moe_w2=moe_w2.astype(BF16),
        g_final=norm_final.reshape(1, D_MODEL).astype(F32),
    )
    LW = [_layer_weights(l, P) for l in range(depth)]

    mem_flat = mem_prompt.reshape(batch * MEM_LEN, D_MODEL)
    mk, mv = _mem_kv(mem_flat, norm_mem.reshape(depth, 1, D_MODEL).astype(F32), xa_wk.astype(BF16),
                     xa_wv.astype(BF16), 512)
    mem_k_p = mk.reshape(depth, batch, MEM_LEN, XA_HEADS, XA_DH)
    mem_v_p = mv.reshape(depth, batch, MEM_LEN, XA_HEADS, XA_DH)
    y_prompt, gla_p, hgrn_p, rglru_p, conv_p = _trunk(
        x_prompt, mk.reshape(depth * batch, MEM_LEN, D_MODEL), mv.reshape(depth * batch, MEM_LEN, D_MODEL),
        None, None, None, None, P, LW)
    dec_batch = x_sample.shape[0]
    y_sample, gla_s, hgrn_s, rglru_s, conv_s = _trunk(
        x_sample, cache_mem_k.reshape(depth * dec_batch, MEM_LEN, D_MODEL),
        cache_mem_v.reshape(depth * dec_batch, MEM_LEN, D_MODEL), state_gla, state_hgrn, state_rglru, state_conv,
        P, LW)
    return (y_prompt, y_sample, gla_p, hgrn_p, rglru_p, conv_p, mem_k_p, mem_v_p, gla_s, hgrn_s, rglru_s, conv_s)
```

```python
import functools

import jax
import jax.numpy as jnp
from jax import lax
from jax.experimental import pallas as pl
from jax.experimental.pallas import tpu as pltpu

F32 = jnp.float32
BF16 = jnp.bfloat16

D_MODEL = 1024
GLA_HEADS = 4
GLA_DK = 48
GLA_DV = 96
GLA_RANK = 16
GLA_TAU = 16.0
HG_HEADS = 4
HG_DK = 64
HG_DV = 96
RG_WIDTH = 256
RG_BLOCKS = 4
RG_CONV = 4
RG_C = 8.0
MEM_LEN = 256
XA_HEADS = 4
XA_DH = D_MODEL // XA_HEADS
D_FF = 2816
N_EXPERTS = 8
EPS = 1e-6
IN_SIZES = (192, 192, 384, 384, 16, 256, 256, 384, 384, 256, 256)

LANE = 128
SUBLANE = 8
VMEM_LIMIT_BYTES = 56 * 1024 * 1024

HEADS = 4
HEAD_K = 64
KL = HEADS * HEAD_K
HEAD_V = 96
VL = HEADS * HEAD_V
SUB = 16
GA_PAD = LANE

OFF_GQ = 0
OFF_GK = OFF_GQ + KL
OFF_GA = OFF_GK + KL
OFF_GV = OFF_GA + GA_PAD
OFF_GR = OFF_GV + VL
OFF_HQ = OFF_GR + VL
OFF_HF = OFF_HQ + KL
OFF_HI = OFF_HF + KL
OFF_HG = OFF_HI + VL
OFF_RX = OFF_HG + VL
OFF_RG = OFF_RX + RG_WIDTH
PROJ_W = OFF_RG + RG_WIDTH


def _cparams(*sem):
    return pltpu.CompilerParams(dimension_semantics=sem, vmem_limit_bytes=VMEM_LIMIT_BYTES)


def _dot(a, b):
    return jnp.dot(a, b, preferred_element_type=F32)


def _dot_nt(a, b):
    return lax.dot_general(a, b, (((1,), (1,)), ((), ())), preferred_element_type=F32)


def _dot_tn(a, b):
    return lax.dot_general(a, b, (((0,), (0,)), ((), ())), preferred_element_type=F32)


def _sigmoid(x):
    return 1.0 / (1.0 + jnp.exp(-x))


def _silu(x):
    return x * _sigmoid(x)


def _softplus(x):
    return jnp.maximum(x, 0.0) + jnp.log1p(jnp.exp(-jnp.abs(x)))


def _expm1(x):
    return jnp.tanh(0.5 * x) * (jnp.exp(x) + 1.0)


def _rms(x, g):
    return x * lax.rsqrt(jnp.mean(x * x, axis=-1, keepdims=True) + EPS) * g


def _split3(x):
    hi = x.astype(BF16)
    r1 = x - hi.astype(F32)
    mid = r1.astype(BF16)
    lo = (r1 - mid.astype(F32)).astype(BF16)
    return hi, mid, lo


def _proj_body(x_ref, g_ref, w_ref, o_ref):
    h = _rms(x_ref[...], g_ref[...]).astype(BF16)
    o_ref[...] = _dot(h, w_ref[...])


def _proj(x, g, w, tm):
    m, d = x.shape
    n = w.shape[1]
    return pl.pallas_call(
        _proj_body,
        grid=(m // tm,),
        in_specs=[
            pl.BlockSpec((tm, d), lambda i: (i, 0)),
            pl.BlockSpec((1, d), lambda i: (0, 0)),
            pl.BlockSpec((d, n), lambda i: (0, 0)),
        ],
        out_specs=pl.BlockSpec((tm, n), lambda i: (i, 0)),
        out_shape=jax.ShapeDtypeStruct((m, n), F32),
        compiler_params=_cparams("parallel"),
        name="in_proj",
    )(x, g, w)


def _memkv_body(m_ref, g_ref, wk_ref, wv_ref, k_ref, v_ref):
    h = _rms(m_ref[...], g_ref[...]).astype(BF16)
    k_ref[...] = _dot(h, wk_ref[...])
    v_ref[...] = _dot(h, wv_ref[...])


def _mem_kv(mem, g, wk, wv, tm):
    m, d = mem.shape
    nl = g.shape[0]
    out = jax.ShapeDtypeStruct((nl, m, d), F32)
    return pl.pallas_call(
        _memkv_body,
        grid=(nl, m // tm),
        in_specs=[
            pl.BlockSpec((tm, d), lambda l, i: (i, 0)),
            pl.BlockSpec((None, 1, d), lambda l, i: (l, 0, 0)),
            pl.BlockSpec((None, d, d), lambda l, i: (l, 0, 0)),
            pl.BlockSpec((None, d, d), lambda l, i: (l, 0, 0)),
        ],
        out_specs=[
            pl.BlockSpec((None, tm, d), lambda l, i: (l, i, 0)),
            pl.BlockSpec((None, tm, d), lambda l, i: (l, i, 0)),
        ],
        out_shape=[out, out],
        compiler_params=_cparams("parallel", "parallel"),
        name="mem_kv",
    )(mem, g, wk, wv)


def _key_head(shape, dim):
    return lax.shift_right_logical(lax.broadcasted_iota(jnp.int32, shape, dim), 6)


def _val_head(shape, dim):
    i = lax.broadcasted_iota(jnp.int32, shape, dim)
    one = jnp.ones(shape, jnp.int32)
    zero = jnp.zeros(shape, jnp.int32)
    return (jnp.where(i >= HEAD_V, one, zero) + jnp.where(i >= 2 * HEAD_V, one, zero)
            + jnp.where(i >= 3 * HEAD_V, one, zero))


def _gated_chunk(q, k, v, lg, st_ref, c):
    nsub = c // SUB
    row = lax.broadcasted_iota(jnp.int32, (c, c), 0)
    col = lax.broadcasted_iota(jnp.int32, (c, c), 1)
    ltri = jnp.where(col <= row, 1.0, 0.0).astype(BF16)
    hi, mid, lo = _split3(lg)
    b = _dot(ltri, hi) + _dot(ltri, mid) + _dot(ltri, lo)
    b_last = b[c - 1:c, :]

    s_t = st_ref[...]
    o_inter = _dot_nt((q * jnp.exp(b)).astype(BF16), s_t.astype(BF16))
    k_end = (k * jnp.exp(b_last - b)).astype(BF16)
    upd = _dot_tn(v.astype(BF16), k_end)
    same_head = _val_head((VL, KL), 0) == _key_head((VL, KL), 1)
    st_ref[...] = s_t * jnp.exp(b_last) + jnp.where(same_head, upd, 0.0)

    khead = _key_head((SUB, KL), 1)
    vhead = _val_head((SUB, VL), 1)
    rr = lax.broadcasted_iota(jnp.int32, (SUB, HEADS * SUB), 0)
    cc = lax.broadcasted_iota(jnp.int32, (SUB, HEADS * SUB), 1)
    causal = jnp.bitwise_and(cc, SUB - 1) <= rr
    pieces = [o_inter[i * SUB:(i + 1) * SUB, :] for i in range(nsub)]
    for j in range(nsub):
        r0 = j * SUB
        e_j = b[r0 + SUB - 1:r0 + SUB, :]
        e_prev = b[r0 - 1:r0, :] if j else jnp.zeros_like(e_j)
        m = 0.5 * (e_prev + e_j)
        lhs = (q[r0:, :] * jnp.exp(b[r0:, :] - m)).astype(BF16)
        ku = k[r0:r0 + SUB, :] * jnp.exp(m - b[r0:r0 + SUB, :])
        ks = jnp.concatenate([jnp.where(khead == h, ku, 0.0) for h in range(HEADS)], axis=0).astype(BF16)
        p = _dot_nt(lhs, ks)
        vj = v[r0:r0 + SUB, :]
        vs = jnp.concatenate([jnp.where(vhead == h, vj, 0.0) for h in range(HEADS)], axis=0).astype(BF16)
        p_diag = jnp.where(causal, p[:SUB, :], 0.0)
        if c - r0 > SUB:
            p = jnp.concatenate([p_diag, p[SUB:, :]], axis=0)
        else:
            p = p_diag
        pv = _dot(p.astype(BF16), vs)
        for i in range(j, nsub):
            pieces[i] = pieces[i] + pv[(i - j) * SUB:(i - j + 1) * SUB, :]
    return jnp.concatenate(pieces, axis=0) if nsub > 1 else pieces[0]


def _head_rms(o, gain):
    seg = (_val_head((VL, VL), 0) == _val_head((VL, VL), 1))
    ones = jnp.where(seg, 1.0, 0.0).astype(BF16)
    sq = o * o
    hi = sq.astype(BF16)
    lo = (sq - hi.astype(F32)).astype(BF16)
    ms = (_dot(hi, ones) + _dot(lo, ones)) * (1.0 / HEAD_V)
    return o * lax.rsqrt(ms + EPS) * gain


def _mixer_body(p_ref, wa2_ref, ba_ref, gon_ref, lb_ref, hon_ref, cw_ref, cb_ref, wr_ref, br_ref, wi_ref,
                bi_ref, lam_ref, sg0_ref, sh0_ref, hr0_ref, cv0_ref,
                mix_ref, sg_ref, sh_ref, hr_ref, cv_ref,
                sg_t, sh_t, h_car, xc, *, c):
    t = pl.program_id(1)
    nt = pl.num_programs(1)
    npad = SUBLANE
    ncar = RG_CONV - 1

    @pl.when(t == 0)
    def _():
        sg_t[...] = sg0_ref[...]
        sh_t[...] = sh0_ref[...]
        h_car[...] = hr0_ref[...]
        xc[npad - ncar:npad, :] = cv0_ref[...]

    gq = p_ref[:, OFF_GQ:OFF_GQ + KL] * (GLA_DK ** -0.5)
    gk = p_ref[:, OFF_GK:OFF_GK + KL]
    ga = p_ref[:, OFF_GA:OFF_GA + GA_PAD]
    z = _dot(ga.astype(BF16), wa2_ref[...]) + ba_ref[...]
    lg = -_softplus(-z) * (1.0 / GLA_TAU)
    o = _gated_chunk(gq, gk, p_ref[:, OFF_GV:OFF_GV + VL], lg, sg_t, c)
    o = _head_rms(o, gon_ref[...]) * _silu(p_ref[:, OFF_GR:OFF_GR + VL])
    mix_ref[:, 0:VL] = o.astype(BF16)

    z = p_ref[:, OFF_HF:OFF_HF + KL]
    lb = lb_ref[...]
    f = lb + (1.0 - lb) * _sigmoid(z)
    hk = (1.0 - lb) * _sigmoid(-z)
    hq = _silu(p_ref[:, OFF_HQ:OFF_HQ + KL])
    o = _gated_chunk(hq, hk, p_ref[:, OFF_HI:OFF_HI + VL], jnp.log(f), sh_t, c)
    o = _head_rms(o, hon_ref[...]) * _silu(p_ref[:, OFF_HG:OFF_HG + VL])
    mix_ref[:, VL:2 * VL] = o.astype(BF16)

    xb = p_ref[:, OFF_RX:OFF_RX + RG_WIDTH]
    xc[npad:npad + c, :] = xb
    u = cb_ref[...]
    for j in range(RG_CONV):
        u = u + xc[npad - ncar + j:npad - ncar + j + c, :] * cw_ref[j:j + 1, :]
    xc[npad - ncar:npad, :] = xc[npad + c - ncar:npad + c, :]
    ub = u.astype(BF16)
    r = _sigmoid(_dot(ub, wr_ref[...]) + br_ref[...])
    ig = _sigmoid(_dot(ub, wi_ref[...]) + bi_ref[...])
    log_a = (-RG_C * r) * _softplus(-lam_ref[...])
    a = jnp.exp(log_a)
    beta = jnp.sqrt(jnp.maximum(-_expm1(2.0 * log_a), 1e-12))
    bt = beta * (ig * u)
    rows = lax.broadcasted_iota(jnp.int32, (c, RG_WIDTH), 0)
    d = 1
    while d < c:
        keep = rows >= d
        a_sh = jnp.where(keep, pltpu.roll(a, d, 0), 1.0)
        b_sh = jnp.where(keep, pltpu.roll(bt, d, 0), 0.0)
        bt = a * b_sh + bt
        a = a * a_sh
        d *= 2
    h = bt + a * h_car[...]
    h_car[...] = h[c - 1:c, :]
    gb = p_ref[:, OFF_RG:OFF_RG + RG_WIDTH]
    gelu = gb * (0.5 * (1.0 + jnp.tanh(0.7978845608028654 * (gb + 0.044715 * (gb * gb * gb)))))
    mix_ref[:, 2 * VL:2 * VL + RG_WIDTH] = (h * gelu).astype(BF16)

    @pl.when(t == nt - 1)
    def _():
        sg_ref[...] = sg_t[...]
        sh_ref[...] = sh_t[...]
        hr_ref[...] = h_car[...]
        cv_ref[...] = xc[npad - ncar:npad, :]


def _mixer(p, lw, sg0, sh0, hr0, cv0, batch, seq, c):
    nt = seq // c
    vec = lambda n: pl.BlockSpec((1, n), lambda b, t: (0, 0))
    mat = lambda r, n: pl.BlockSpec((r, n), lambda b, t: (0, 0))
    st = lambda r, n: pl.BlockSpec((None, r, n), lambda b, t: (b, 0, 0))
    in_specs = [
        pl.BlockSpec((c, PROJ_W), lambda b, t: (b * nt + t, 0)),
        mat(GA_PAD, KL), vec(KL), vec(VL), vec(KL), vec(VL), mat(SUBLANE, RG_WIDTH), vec(RG_WIDTH),
        mat(RG_WIDTH, RG_WIDTH), vec(RG_WIDTH), mat(RG_WIDTH, RG_WIDTH), vec(RG_WIDTH), vec(RG_WIDTH),
        st(VL, KL), st(VL, KL), st(1, RG_WIDTH), st(RG_CONV - 1, RG_WIDTH),
    ]
    out_specs = [
        pl.BlockSpec((c, D_MODEL), lambda b, t: (b * nt + t, 0)),
        st(VL, KL), st(VL, KL), st(1, RG_WIDTH), st(RG_CONV - 1, RG_WIDTH),
    ]
    out_shape = [
        jax.ShapeDtypeStruct((batch * seq, D_MODEL), BF16),
        jax.ShapeDtypeStruct((batch, VL, KL), F32),
        jax.ShapeDtypeStruct((batch, VL, KL), F32),
        jax.ShapeDtypeStruct((batch, 1, RG_WIDTH), F32),
        jax.ShapeDtypeStruct((batch, RG_CONV - 1, RG_WIDTH), F32),
    ]
    scratch = [
        pltpu.VMEM((VL, KL), F32), pltpu.VMEM((VL, KL), F32), pltpu.VMEM((1, RG_WIDTH), F32),
        pltpu.VMEM((c + SUBLANE, RG_WIDTH), F32),
    ]
    return pl.pallas_call(
        functools.partial(_mixer_body, c=c),
        grid=(batch, nt),
        in_specs=in_specs,
        out_specs=out_specs,
        out_shape=out_shape,
        scratch_shapes=scratch,
        compiler_params=_cparams("parallel", "arbitrary"),
        name="mixer",
    )(p, lw["wa2"], lw["ba"], lw["gla_on"], lw["lb"], lw["hg_on"], lw["conv_w"], lw["conv_b"], lw["wr"],
      lw["br"], lw["wi"], lw["bi"], lw["lam"], sg0, sh0, hr0, cv0)


ROUTE_I1, ROUTE_I2, ROUTE_G1, ROUTE_G2, ROUTE_R1, ROUTE_R2 = range(6)


def _attn_body(*refs, with_router):
    if with_router:
        (x_ref, mix_ref, wout_ref, gx_ref, wq_ref, k_ref, v_ref, wo_ref, gf_ref, rhi_ref, rlo_ref,
         x2_ref, hb_ref, route_ref, cnt_ref, run_cnt) = refs
    else:
        x_ref, mix_ref, wout_ref, gx_ref, wq_ref, k_ref, v_ref, wo_ref, gf_ref, x2_ref, hb_ref = refs
    x1 = x_ref[...] + _dot(mix_ref[...], wout_ref[...])
    h = _rms(x1, gx_ref[...]).astype(BF16)
    q = _dot(h, wq_ref[...])
    outs = []
    for hd in range(XA_HEADS):
        sl = slice(hd * XA_DH, (hd + 1) * XA_DH)
        s = _dot_nt(q[:, sl].astype(BF16), k_ref[:, sl].astype(BF16)) * (XA_DH ** -0.5)
        e = jnp.exp(s - jnp.max(s, axis=-1, keepdims=True))
        p = e / jnp.sum(e, axis=-1, keepdims=True)
        outs.append(_dot(p.astype(BF16), v_ref[:, sl].astype(BF16)))
    o = jnp.concatenate(outs, axis=-1).astype(BF16)
    x2 = x1 + _dot(o, wo_ref[...])
    x2_ref[...] = x2
    h2 = _rms(x2, gf_ref[...])
    h2_hi = h2.astype(BF16)
    if not with_router:
        hb_ref[...] = h2_hi
        return
    hb_ref[...] = h2

    @pl.when((pl.program_id(0) == 0) & (pl.program_id(1) == 0))
    def _():
        run_cnt[...] = jnp.zeros_like(run_cnt)

    h2_lo = (h2 - h2_hi.astype(F32)).astype(BF16)
    logits = _dot(h2_hi, rhi_ref[...]) + _dot(h2_lo, rhi_ref[...]) + _dot(h2_hi, rlo_ref[...])
    tq = logits.shape[0]
    lane = lax.broadcasted_iota(jnp.int32, logits.shape, 1).astype(F32)
    neg = -jnp.inf
    lg = jnp.where(lane < N_EXPERTS, logits, neg)
    m1 = jnp.max(lg, axis=-1, keepdims=True)
    i1 = jnp.min(jnp.where(lg == m1, lane, float(LANE)), axis=-1, keepdims=True)
    lg2 = jnp.where(lane == i1, neg, lg)
    m2 = jnp.max(lg2, axis=-1, keepdims=True)
    i2 = jnp.min(jnp.where(lg2 == m2, lane, float(LANE)), axis=-1, keepdims=True)
    e2 = jnp.exp(m2 - m1)
    g1 = 1.0 / (1.0 + e2)
    g2 = e2 / (1.0 + e2)
    sel = jnp.maximum(jnp.where(lane == i1, 1.0, 0.0), jnp.where(lane == i2, 1.0, 0.0))
    rr = lax.broadcasted_iota(jnp.int32, (tq, tq), 0)
    cc = lax.broadcasted_iota(jnp.int32, (tq, tq), 1)
    before = jnp.where(cc < rr, 1.0, 0.0).astype(BF16)
    rank = _dot(before, sel.astype(BF16)) + run_cnt[...]
    r1 = jnp.sum(jnp.where(lane == i1, rank, 0.0), axis=-1, keepdims=True)
    r2 = jnp.sum(jnp.where(lane == i2, rank, 0.0), axis=-1, keepdims=True)
    run_cnt[...] = run_cnt[...] + jnp.sum(sel, axis=0, keepdims=True)
    cnt_ref[...] = run_cnt[...]
    rec = jnp.zeros_like(logits)
    for ln, val in ((ROUTE_I1, i1), (ROUTE_I2, i2), (ROUTE_G1, g1), (ROUTE_G2, g2), (ROUTE_R1, r1), (ROUTE_R2, r2)):
        rec = jnp.where(lane == float(ln), val, rec)
    route_ref[...] = rec


def _attn(x, mix, mk, mv, layer, lw, g_ffn, router, batch, seq, tq):
    m = batch * seq
    nt = seq // tq
    with_router = router is not None
    row = pl.BlockSpec((tq, D_MODEL), lambda b, t: (b * nt + t, 0))
    vec = pl.BlockSpec((1, D_MODEL), lambda b, t: (0, 0))
    mat = pl.BlockSpec((D_MODEL, D_MODEL), lambda b, t: (0, 0))
    kv = pl.BlockSpec((None, MEM_LEN, D_MODEL), lambda b, t: (layer * batch + b, 0, 0))
    in_specs = [row, row, mat, vec, mat, kv, kv, mat, vec]
    args = [x, mix, lw["w_out"], lw["g_xattn"], lw["wq"], mk, mv, lw["wo"], g_ffn]
    out_specs = [row, row]
    out_shape = [jax.ShapeDtypeStruct((m, D_MODEL), F32),
                 jax.ShapeDtypeStruct((m, D_MODEL), F32 if with_router else BF16)]
    scratch = []
    if with_router:
        rspec = pl.BlockSpec((D_MODEL, LANE), lambda b, t: (0, 0))
        in_specs += [rspec, rspec]
        args += [router[0], router[1]]
        out_specs += [pl.BlockSpec((tq, LANE), lambda b, t: (b * nt + t, 0)),
                      pl.BlockSpec((1, LANE), lambda b, t: (0, 0))]
        out_shape += [jax.ShapeDtypeStruct((m, LANE), F32), jax.ShapeDtypeStruct((1, LANE), F32)]
        scratch = [pltpu.VMEM((1, LANE), F32)]
    sem = ("arbitrary", "arbitrary") if with_router else ("parallel", "parallel")
    return pl.pallas_call(
        functools.partial(_attn_body, with_router=with_router),
        grid=(batch, nt),
        in_specs=in_specs,
        out_specs=out_specs,
        out_shape=out_shape,
        scratch_shapes=scratch,
        compiler_params=_cparams(*sem),
        name="xattn_router" if with_router else "xattn",
    )(*args)


FF_CHUNK = 256


def _swiglu_tile(h, w1_ref, w3_ref, w2_ref):
    y = None
    for f0 in range(0, D_FF, FF_CHUNK):
        a = _dot(h, w1_ref[:, f0:f0 + FF_CHUNK])
        c = _dot(h, w3_ref[:, f0:f0 + FF_CHUNK])
        g = (_silu(a) * c).astype(BF16)
        part = _dot(g, w2_ref[f0:f0 + FF_CHUNK, :])
        y = part if y is None else y + part
    return y


def _ffn_body(hb_ref, x_ref, w1_ref, w3_ref, w2_ref, gfin_ref, o_ref, *, final_norm):
    out = x_ref[...] + _swiglu_tile(hb_ref[...], w1_ref, w3_ref, w2_ref)
    o_ref[...] = _rms(out, gfin_ref[...]) if final_norm else out


def _ffn(hb, x, w1, w3, w2, g_final, tm, final_norm):
    m = x.shape[0]
    row = pl.BlockSpec((tm, D_MODEL), lambda i: (i, 0))
    return pl.pallas_call(
        functools.partial(_ffn_body, final_norm=final_norm),
        grid=(m // tm,),
        in_specs=[
            row, row,
            pl.BlockSpec((D_MODEL, D_FF), lambda i: (0, 0)),
            pl.BlockSpec((D_MODEL, D_FF), lambda i: (0, 0)),
            pl.BlockSpec((D_FF, D_MODEL), lambda i: (0, 0)),
            pl.BlockSpec((1, D_MODEL), lambda i: (0, 0)),
        ],
        out_specs=row,
        out_shape=jax.ShapeDtypeStruct((m, D_MODEL), F32),
        compiler_params=_cparams("parallel"),
        name="ffn",
    )(hb, x, w1, w3, w2, g_final)


ISSUE_UNROLL = 8


def _route_plan(route, cnt, m, tm):
    i1 = route[:, ROUTE_I1].astype(jnp.int32)
    i2 = route[:, ROUTE_I2].astype(jnp.int32)
    r1 = route[:, ROUTE_R1].astype(jnp.int32)
    r2 = route[:, ROUTE_R2].astype(jnp.int32)
    counts = cnt[0, :N_EXPERTS].astype(jnp.int32)
    padded = ((counts + tm - 1) // tm) * tm
    off_end = jnp.cumsum(padded)
    off = off_end - padded
    pos1 = jnp.take(off, i1) + r1
    pos2 = jnp.take(off, i2) + r2
    n_tiles = (2 * m) // tm + N_EXPERTS
    n_used = off_end[-1] // tm
    tile = jnp.minimum(jnp.arange(n_tiles, dtype=jnp.int32), n_used - 1)
    tile_expert = jnp.sum((tile[:, None] * tm >= off_end[None, :]).astype(jnp.int32), axis=1)
    return dict(pos1=pos1, pos2=pos2, tile_expert=tile_expert, n_used=n_used.reshape(1),
                pad_start=off + counts, pad_len=padded - counts, n_tiles=n_tiles, n_slots=n_tiles * tm)


def _dispatch_body(pad_start, pad_len, n_used, pos1_ref, pos2_ref, h_ref, xs_hbm, zrow, sem, zsem, *, tb, n_tiles):
    i = pl.program_id(0)

    def row_copy(r, p):
        return pltpu.make_async_copy(h_ref.at[pl.ds(r, 1)], xs_hbm.at[pl.ds(p, 1)], sem)

    def issue(r, c):
        row_copy(r, pos1_ref[r]).start(priority=0)
        row_copy(r, pos2_ref[r]).start(priority=1)
        return c

    lax.fori_loop(0, tb, issue, 0, unroll=ISSUE_UNROLL)

    @pl.when(i == 0)
    def _():
        zrow[...] = jnp.zeros_like(zrow)

        def zero_copy(p):
            return pltpu.make_async_copy(zrow.at[pl.ds(0, 1)], xs_hbm.at[pl.ds(p, 1)], zsem)

        def zero_tile(t):
            return pltpu.make_async_copy(zrow, xs_hbm.at[pl.ds(t * tb, tb)], zsem)

        for e in range(N_EXPERTS):
            lax.fori_loop(0, pad_len[e], lambda r, c: (zero_copy(pad_start[e] + r).start(), c)[1], 0)
        lax.fori_loop(n_used[0], n_tiles, lambda t, c: (zero_tile(t).start(), c)[1], 0)
        for e in range(N_EXPERTS):
            lax.fori_loop(0, pad_len[e], lambda r, c: (zero_copy(pad_start[e] + r).wait(), c)[1], 0)
        lax.fori_loop(n_used[0], n_tiles, lambda t, c: (zero_tile(t).wait(), c)[1], 0)

    pltpu.make_async_copy(h_ref, xs_hbm.at[pl.ds(0, tb)], sem).wait()
    pltpu.make_async_copy(h_ref, xs_hbm.at[pl.ds(0, tb)], sem).wait()


def _dispatch(h2, plan, tb):
    m = h2.shape[0]
    ns = plan["n_slots"]
    idx = pl.BlockSpec((tb,), lambda i, *_: (i,), memory_space=pltpu.SMEM)
    return pl.pallas_call(
        functools.partial(_dispatch_body, tb=tb, n_tiles=plan["n_tiles"]),
        grid_spec=pltpu.PrefetchScalarGridSpec(
            num_scalar_prefetch=3,
            grid=(m // tb,),
            in_specs=[idx, idx, pl.BlockSpec((tb, D_MODEL), lambda i, *_: (i, 0))],
            out_specs=pl.BlockSpec(memory_space=pl.ANY),
            scratch_shapes=[pltpu.VMEM((tb, D_MODEL), F32), pltpu.SemaphoreType.DMA(()),
                            pltpu.SemaphoreType.DMA(())],
        ),
        out_shape=jax.ShapeDtypeStruct((ns, D_MODEL), F32),
        compiler_params=_cparams("arbitrary"),
        name="moe_dispatch",
    )(plan["pad_start"], plan["pad_len"], plan["n_used"], plan["pos1"], plan["pos2"], h2)


def _expert_body(tile_expert, n_used, xs_ref, w1_ref, w3_ref, w2_ref, ys_ref):
    used = pl.program_id(0) < n_used[0]

    @pl.when(used)
    def _():
        ys_ref[...] = _swiglu_tile(xs_ref[...].astype(BF16), w1_ref, w3_ref, w2_ref)

    @pl.when(jnp.logical_not(used))
    def _():
        ys_ref[...] = jnp.zeros_like(ys_ref)


def _expert_ffn(xs, plan, w1, w3, w2, tm):
    ns = xs.shape[0]
    row = pl.BlockSpec((tm, D_MODEL), lambda i, te, nu: (jnp.minimum(i, nu[0] - 1), 0))
    out_row = pl.BlockSpec((tm, D_MODEL), lambda i, te, nu: (i, 0))
    return pl.pallas_call(
        _expert_body,
        grid_spec=pltpu.PrefetchScalarGridSpec(
            num_scalar_prefetch=2,
            grid=(plan["n_tiles"],),
            in_specs=[
                row,
                pl.BlockSpec((None, D_MODEL, D_FF), lambda i, te, nu: (te[i], 0, 0)),
                pl.BlockSpec((None, D_MODEL, D_FF), lambda i, te, nu: (te[i], 0, 0)),
                pl.BlockSpec((None, D_FF, D_MODEL), lambda i, te, nu: (te[i], 0, 0)),
            ],
            out_specs=out_row,
        ),
        out_shape=jax.ShapeDtypeStruct((ns, D_MODEL), F32),
        compiler_params=_cparams("arbitrary"),
        name="moe_experts",
    )(plan["tile_expert"], plan["n_used"], xs, w1, w3, w2)


def _combine_body(pos1_ref, pos2_ref, route_ref, x_ref, gfin_ref, ys_hbm, o_ref, ya, yb, sem, *, tb, final_norm):
    def issue(r, c):
        pltpu.make_async_copy(ys_hbm.at[pl.ds(pos1_ref[r], 1)], ya.at[pl.ds(r, 1)], sem).start()
        pltpu.make_async_copy(ys_hbm.at[pl.ds(pos2_ref[r], 1)], yb.at[pl.ds(r, 1)], sem).start()
        return c

    lax.fori_loop(0, tb, issue, 0, unroll=ISSUE_UNROLL)
    pltpu.make_async_copy(ys_hbm.at[pl.ds(0, tb)], ya, sem).wait()
    pltpu.make_async_copy(ys_hbm.at[pl.ds(0, tb)], yb, sem).wait()
    rec = route_ref[...]
    lane = lax.broadcasted_iota(jnp.int32, rec.shape, 1)
    g1 = jnp.sum(jnp.where(lane == ROUTE_G1, rec, 0.0), axis=-1, keepdims=True)
    g2 = jnp.sum(jnp.where(lane == ROUTE_G2, rec, 0.0), axis=-1, keepdims=True)
    out = x_ref[...] + (g1 * ya[...] + g2 * yb[...])
    o_ref[...] = _rms(out, gfin_ref[...]) if final_norm else out


def _combine(x, ys, route, plan, g_final, tb, final_norm):
    m = x.shape[0]
    idx = pl.BlockSpec((tb,), lambda i: (i,), memory_space=pltpu.SMEM)
    row = pl.BlockSpec((tb, D_MODEL), lambda i: (i, 0))
    return pl.pallas_call(
        functools.partial(_combine_body, tb=tb, final_norm=final_norm),
        grid=(m // tb,),
        in_specs=[idx, idx, pl.BlockSpec((tb, LANE), lambda i: (i, 0)), row,
                  pl.BlockSpec((1, D_MODEL), lambda i: (0, 0)), pl.BlockSpec(memory_space=pl.ANY)],
        out_specs=row,
        out_shape=jax.ShapeDtypeStruct((m, D_MODEL), F32),
        scratch_shapes=[pltpu.VMEM((tb, D_MODEL), F32), pltpu.VMEM((tb, D_MODEL), F32),
                        pltpu.SemaphoreType.DMA(())],
        compiler_params=_cparams("arbitrary"),
        name="moe_combine",
    )(plan["pos1"], plan["pos2"], route, x, g_final, ys)


def _moe(h2, x, route, cnt, w1, w3, w2, g_final, tm, final_norm):
    m = x.shape[0]
    plan = _route_plan(route, cnt, m, tm)
    xs = _dispatch(h2, plan, tm)
    ys = _expert_ffn(xs, plan, w1, w3, w2, tm)
    return _combine(x, ys, route, plan, g_final, tm, final_norm)


def _pad_heads(w, dk):
    lead = w.shape[:-1]
    w = w.reshape(lead + (HEADS, dk))
    w = jnp.pad(w, [(0, 0)] * len(lead) + [(0, 0), (0, HEAD_K - dk)])
    return w.reshape(lead + (KL,))


def _block_diag(w):
    nb, n, _ = w.shape
    eye = jnp.eye(nb, dtype=w.dtype)
    return jnp.einsum("ncd,nm->ncmd", w, eye).reshape(nb * n, nb * n)


def _layer_weights(l, P):
    offs = [0]
    for s in IN_SIZES:
        offs.append(offs[-1] + s)
    cols = [P["w_in"][l][:, offs[i]:offs[i + 1]] for i in range(len(IN_SIZES))]
    gq, gk, gv, gr, ga, hq, hf, hi, hg, rx, rg = cols
    w_in = jnp.concatenate([
        _pad_heads(gq, GLA_DK), _pad_heads(gk, GLA_DK), jnp.pad(ga, ((0, 0), (0, GA_PAD - GLA_RANK))), gv, gr,
        hq, hf, hi, hg, rx, rg], axis=1).astype(BF16)
    row = lambda v: v.reshape(1, -1).astype(F32)
    return dict(
        g_mix=row(P["norm_mix"][l]), w_in=w_in,
        wa2=jnp.pad(_pad_heads(P["gla_wa2"][l], GLA_DK), ((0, GA_PAD - GLA_RANK), (0, 0))).astype(BF16),
        ba=row(_pad_heads(P["gla_ba"][l], GLA_DK)), gla_on=row(P["gla_onorm"][l]),
        lb=row(P["lb_all"][l]), hg_on=row(P["hg_onorm"][l]),
        conv_w=jnp.pad(P["rg_conv_w"][l], ((0, SUBLANE - RG_CONV), (0, 0))).astype(F32),
        conv_b=row(P["rg_conv_b"][l]),
        wr=_block_diag(P["rg_wr"][l]).astype(BF16), br=row(P["rg_br"][l]),
        wi=_block_diag(P["rg_wi"][l]).astype(BF16), bi=row(P["rg_bi"][l]),
        lam=row(P["rg_lambda"][l]),
        w_out=P["w_out"][l].astype(BF16), g_xattn=row(P["norm_xattn"][l]),
        wq=P["xa_wq"][l].astype(BF16), wo=P["xa_wo"][l].astype(BF16),
        g_ffn=row(P["norm_ffn"][l]),
    )


def _state_to_kernel(s, dk):
    s = jnp.pad(s, ((0, 0), (0, 0), (0, HEAD_K - dk), (0, 0)))
    eye = jnp.eye(HEADS, dtype=s.dtype)
    b = s.shape[0]
    return jnp.einsum("bhkv,hg->bhvgk", s, eye).reshape(b, VL, KL)


def _state_from_kernel(st, dk):
    b = st.shape[0]
    s = st.reshape(b, HEADS, HEAD_V, HEADS, HEAD_K)
    s = jnp.stack([s[:, h, :, h, :] for h in range(HEADS)], axis=1)
    return jnp.swapaxes(s, -1, -2)[:, :, :dk, :]


def _trunk(x, mem_k, mem_v, s_gla, s_hg, s_rg, s_conv, P, LW):
    batch, seq, _ = x.shape
    depth = len(LW)
    m = batch * seq
    c = min(seq, 128)
    tm = min(m, 512)
    tq = min(seq, 512)
    xf = x.reshape(m, D_MODEL)
    n_gla, n_hg, n_rg, n_cv = [], [], [], []
    for l in range(depth):
        lw = LW[l]
        p = _proj(xf, lw["g_mix"], lw["w_in"], tm)
        if s_gla is None:
            sg0 = sh0 = jnp.zeros((batch, VL, KL), F32)
            hr0 = jnp.zeros((batch, 1, RG_WIDTH), F32)
            cv0 = jnp.zeros((batch, RG_CONV - 1, RG_WIDTH), F32)
        else:
            sg0, sh0 = _state_to_kernel(s_gla[l], GLA_DK), _state_to_kernel(s_hg[l], HG_DK)
            hr0, cv0 = s_rg[l].reshape(batch, 1, RG_WIDTH), s_conv[l]
        mix, sg, sh, hr, cv = _mixer(p, lw, sg0, sh0, hr0, cv0, batch, seq, c)
        moe = l % 2 == 1
        j = l // 2
        router = P["router"][j] if moe else None
        res = _attn(xf, mix, mem_k, mem_v, l, lw, lw["g_ffn"], router, batch, seq, tq)
        last = l == depth - 1
        if moe:
            x2, h2, route, cnt = res
            xf = _moe(h2, x2, route, cnt, P["moe_w1"][j], P["moe_w3"][j], P["moe_w2"][j], P["g_final"], tm, last)
        else:
            x2, hb = res
            xf = _ffn(hb, x2, P["ffd_w1"][j], P["ffd_w3"][j], P["ffd_w2"][j], P["g_final"], tm, last)
        n_gla.append(_state_from_kernel(sg, GLA_DK))
        n_hg.append(_state_from_kernel(sh, HG_DK))
        n_rg.append(hr.reshape(batch, RG_WIDTH))
        n_cv.append(cv)
    return (xf.reshape(batch, seq, D_MODEL), jnp.stack(n_gla), jnp.stack(n_hg), jnp.stack(n_rg), jnp.stack(n_cv))


def kernel(x_prompt, x_sample, state_gla, state_hgrn, state_rglru, state_conv, cache_mem_k, cache_mem_v, mem_prompt, norm_mix, norm_xattn, norm_ffn, norm_mem, norm_final, w_in, gla_wa2, gla_ba, gla_onorm, hg_lb, hg_onorm, rg_conv_w, rg_conv_b, rg_wr, rg_br, rg_wi, rg_bi, rg_lambda, w_out, xa_wq, xa_wk, xa_wv, xa_wo, ffd_w1, ffd_w3, ffd_w2, moe_router, moe_w1, moe_w3, moe_w2):
    depth = w_in.shape[0]
    batch = x_prompt.shape[0]
    pl_ = jax.nn.softmax(hg_lb.astype(F32), axis=0)
    lb_all = jnp.cumsum(pl_, axis=0) - pl_[:1]
    r_hi = jnp.pad(moe_router, ((0, 0), (0, 0), (0, LANE - N_EXPERTS)))
    r_hi_b = r_hi.astype(BF16)
    r_lo_b = (r_hi - r_hi_b.astype(F32)).astype(BF16)
    P = dict(
        w_in=w_in, gla_wa2=gla_wa2, gla_ba=gla_ba, gla_onorm=gla_onorm, lb_all=lb_all, hg_onorm=hg_onorm,
        rg_conv_w=rg_conv_w, rg_conv_b=rg_conv_b, rg_wr=rg_wr, rg_br=rg_br, rg_wi=rg_wi, rg_bi=rg_bi,
        rg_lambda=rg_lambda, w_out=w_out, xa_wq=xa_wq, xa_wo=xa_wo, norm_mix=norm_mix, norm_xattn=norm_xattn,
        norm_ffn=norm_ffn,
        router=[(r_hi_b[j], r_lo_b[j]) for j in range(moe_router.shape[0])],
        ffd_w1=ffd_w1.astype(BF16), ffd_w3=ffd_w3.astype(BF16), ffd_w2=ffd_w2.astype(BF16),
        moe_w1=moe_w1.astype(BF16), moe_w3=moe_w3.astype(BF16), moe_w2=moe_w2.astype(BF16),
        g_final=norm_final.reshape(1, D_MODEL).astype(F32),
    )
    LW = [_layer_weights(l, P) for l in range(depth)]

    mem_flat = mem_prompt.reshape(batch * MEM_LEN, D_MODEL)
    mk, mv = _mem_kv(mem_flat, norm_mem.reshape(depth, 1, D_MODEL).astype(F32), xa_wk.astype(BF16),
                     xa_wv.astype(BF16), 512)
    mem_k_p = mk.reshape(depth, batch, MEM_LEN, XA_HEADS, XA_DH)
    mem_v_p = mv.reshape(depth, batch, MEM_LEN, XA_HEADS, XA_DH)
    y_prompt, gla_p, hgrn_p, rglru_p, conv_p = _trunk(
        x_prompt, mk.reshape(depth * batch, MEM_LEN, D_MODEL), mv.reshape(depth * batch, MEM_LEN, D_MODEL),
        None, None, None, None, P, LW)
    dec_batch = x_sample.shape[0]
    y_sample, gla_s, hgrn_s, rglru_s, conv_s = _trunk(
        x_sample, cache_mem_k.reshape(depth * dec_batch, MEM_LEN, D_MODEL),
        cache_mem_v.reshape(depth * dec_batch, MEM_LEN, D_MODEL), state_gla, state_hgrn, state_rglru, state_conv,
        P, LW)
    return (y_prompt, y_sample, gla_p, hgrn_p, rglru_p, conv_p, mem_k_p, mem_v_p, gla_s, hgrn_s, rglru_s, conv_s)
```
